```python
import math
import jax
import jax.numpy as jnp
from jax import lax
import numpy as np

D_MODEL = 2048
BATCH = 16
SEQ = 256
DEPTH = 4
DEC_BATCH = 8
DEC_SEQ = 1024
PAST_LEN = 512

GRID_W = 64
N_MIXERS = 4
GROUP_W = D_MODEL // N_MIXERS
DA_HEADS = 4
DA_DV = GROUP_W // DA_HEADS
DA_DQ = DA_DV // 2
ROPE_BASE = 10000.0
CONV_W = 31
CONV_GROUPS = 4
NA_HEADS = 4
NA_DH = GROUP_W // NA_HEADS
WIN_R = 8
WIN_C = 16
NA_QCOLS = 16
NA_KCOLS = 2 * WIN_C
DN_HEADS = 4
DN_DK = GROUP_W // DN_HEADS
DN_DV = GROUP_W // DN_HEADS
DN_CONV_W = 3
CHUNK = 64
N_GROUPS = 4
E_PER_GROUP = 8
N_EXPERTS = N_GROUPS * E_PER_GROUP
TOP_K = 2
D_FF_EXPERT = D_MODEL // 4
EXPERT_BLOCK = 128
QBLK = 128
EPS = 1e-6
NEG_INF = -1e30
IN_SIZES = (GROUP_W, GROUP_W, GROUP_W,
            2 * GROUP_W,
            GROUP_W, GROUP_W, GROUP_W,
            3 * GROUP_W, GROUP_W,
            2 * DN_HEADS, 2 * DN_HEADS)
N_IN = 12 * GROUP_W + 4 * DN_HEADS

kernel_name = 'hybrid_diffusion_prefix_step'

F32 = jnp.float32


def rms_norm(x, g):
    xf = x.astype(F32)
    y = xf * lax.rsqrt(jnp.mean(xf * xf, axis=-1, keepdims=True) + EPS)
    return y.astype(x.dtype) * g


def l2_normalize(x):
    xf = x.astype(F32)
    return xf * lax.rsqrt(jnp.sum(xf * xf, axis=-1, keepdims=True) + EPS)


def depthwise_conv(x, w):
    k = w.shape[0]
    return lax.conv_general_dilated(x, w[:, None, :], (1,), ((k // 2, k // 2),),
                                    dimension_numbers=('NWC', 'WIO', 'NWC'),
                                    feature_group_count=x.shape[-1])


def axial_rope(x):
    L = x.shape[1]
    t = jnp.arange(L)
    nf = DA_DQ // 4
    freqs = ROPE_BASE ** (-jnp.arange(nf, dtype=F32) / nf)

    def rot(xa, pos):
        ang = pos.astype(F32)[:, None] * freqs[None]
        cos = jnp.cos(ang)[None, :, None, :].astype(x.dtype)
        sin = jnp.sin(ang)[None, :, None, :].astype(x.dtype)
        x1, x2 = xa[..., :nf], xa[..., nf:]
        return jnp.concatenate([x1 * cos - x2 * sin, x1 * sin + x2 * cos], axis=-1)

    half = DA_DQ // 2
    return jnp.concatenate([rot(x[..., :half], t // GRID_W), rot(x[..., half:], t % GRID_W)], axis=-1)


def rope_pairs(t):
    B, L, H, _ = t.shape
    return axial_rope(t.reshape(B, L, 2 * H, DA_DQ)).reshape(B, L, H, 2 * DA_DQ)


def _query_blocks(q):
    B, L, H, d = q.shape
    return q.reshape(B, L // QBLK, QBLK, H, d).transpose(1, 0, 2, 3, 4)


def _merge_blocks(o):
    nb, B, Q, H, d = o.shape
    return o.transpose(1, 0, 2, 3, 4).reshape(B, nb * Q, H, d)


def softmax_attention(q, k, v):
    scale = q.shape[-1] ** -0.5

    def one(qb):
        s = jnp.einsum('bqhd,bkhd->bhqk', qb, k).astype(F32) * scale
        p = jax.nn.softmax(s, axis=-1).astype(v.dtype)
        return jnp.einsum('bhqk,bkhd->bqhd', p, v)

    return _merge_blocks(lax.map(one, _query_blocks(q)))


def diff_attention(q1, q2, k1, k2, v, lam):
    scale = q1.shape[-1] ** -0.5

    def one(qs):
        a, b = qs
        p1 = jax.nn.softmax(jnp.einsum('bqhd,bkhd->bhqk', a, k1).astype(F32) * scale, axis=-1)
        p2 = jax.nn.softmax(jnp.einsum('bqhd,bkhd->bhqk', b, k2).astype(F32) * scale, axis=-1)
        return jnp.einsum('bhqk,bkhd->bqhd', (p1 - lam * p2).astype(v.dtype), v)

    return _merge_blocks(lax.map(one, (_query_blocks(q1), _query_blocks(q2))))


def _na_column_blocks():
    n_cb = GRID_W // NA_QCOLS
    cols = np.arange(GRID_W).reshape(n_cb, NA_QCOLS)
    win_start = np.clip(cols - WIN_C // 2, 0, GRID_W - WIN_C)
    key_start = np.minimum(win_start[:, 0], GRID_W - NA_KCOLS)
    key_cols = key_start[:, None] + np.arange(NA_KCOLS)
    mask = ((key_cols[:, None, :] >= win_start[:, :, None])
            & (key_cols[:, None, :] < win_start[:, :, None] + WIN_C))
    col_off = np.clip(key_cols[:, None, :] - cols[:, :, None] + WIN_C - 1, 0, 2 * WIN_C - 2)
    return key_cols, mask, col_off


def neighborhood_attention(q, k, v, k_ctx, v_ctx, rpb):
    B, L, H, dh = q.shape
    rows = L // GRID_W
    kh = min(WIN_R, rows)
    n_cb = GRID_W // NA_QCOLS
    key_cols, mask, col_off = _na_column_blocks()
    key_cols = jnp.asarray(key_cols)
    mask = jnp.asarray(mask)[None, None, :, :, None, :]
    col_off = jnp.asarray(col_off)
    scale = dh ** -0.5
    qg = q.reshape(B, rows, GRID_W, H, dh)
    kg = k.reshape(B, rows, GRID_W, H, dh)
    vg = v.reshape(B, rows, GRID_W, H, dh)
    n_win = kh * NA_KCOLS

    def one_row(r):
        rs = jnp.clip(r - kh // 2, 0, rows - kh)
        kb = lax.dynamic_slice_in_dim(kg, rs, kh, axis=1)[:, :, key_cols]
        vb = lax.dynamic_slice_in_dim(vg, rs, kh, axis=1)[:, :, key_cols]
        qr = lax.dynamic_index_in_dim(qg, r, axis=1, keepdims=False).reshape(B, n_cb, NA_QCOLS, H, dh)
        row_idx = rs + jnp.arange(kh) - r + WIN_R - 1
        bias = rpb[:, row_idx][:, :, col_off].transpose(0, 2, 3, 1, 4)
        s_win = jnp.einsum('bjqhd,bajkhd->bhjqak', qr, kb).astype(F32) * scale + bias.astype(F32)
        s_win = jnp.where(mask, s_win, NEG_INF).reshape(B, H, n_cb, NA_QCOLS, n_win)
        s_ctx = jnp.einsum('bjqhd,bkhd->bhjqk', qr, k_ctx).astype(F32) * scale
        p = jax.nn.softmax(jnp.concatenate([s_win, s_ctx], axis=-1), axis=-1).astype(v.dtype)
        p_win = p[..., :n_win].reshape(B, H, n_cb, NA_QCOLS, kh, NA_KCOLS)
        o = (jnp.einsum('bhjqak,bajkhd->bjqhd', p_win, vb)
             + jnp.einsum('bhjqk,bkhd->bjqhd', p[..., n_win:], v_ctx))
        return o.reshape(B, GRID_W, H, dh)

    o = lax.map(one_row, jnp.arange(rows))
    return o.transpose(1, 0, 2, 3, 4).reshape(B, L, H, dh)


def gated_delta_chunked(q, k, v, g, beta, s0):
    B, L, H, dk = q.shape
    dv = v.shape[-1]
    n = L // CHUNK
    qc = (q.astype(F32) * dk ** -0.5).reshape(B, n, CHUNK, H, dk).transpose(1, 0, 3, 2, 4)
    kc = k.astype(F32).reshape(B, n, CHUNK, H, dk).transpose(1, 0, 3, 2, 4)
    vc = v.astype(F32).reshape(B, n, CHUNK, H, dv).transpose(1, 0, 3, 2, 4)
    gc = jnp.cumsum(g.astype(F32).reshape(B, n, CHUNK, H).transpose(1, 0, 3, 2), axis=-1)
    bc = beta.astype(F32).reshape(B, n, CHUNK, H).transpose(1, 0, 3, 2)
    causal = jnp.tril(jnp.ones((CHUNK, CHUNK), bool))
    strict = jnp.tril(jnp.ones((CHUNK, CHUNK), bool), -1)
    decay = jnp.where(causal, jnp.exp(jnp.where(causal, gc[..., :, None] - gc[..., None, :], 0.0)), 0.0)
    kb = kc * bc[..., None]
    a = jnp.where(strict, jnp.einsum('nbhid,nbhjd->nbhij', kb, kc) * decay, 0.0)
    eye = jnp.eye(CHUNK, dtype=F32)
    t_inv = lax.linalg.triangular_solve(eye + a, jnp.broadcast_to(eye, a.shape), left_side=True, lower=True)
    u = jnp.einsum('nbhij,nbhjd->nbhid', t_inv, vc * bc[..., None])
    w = jnp.einsum('nbhij,nbhjd->nbhid', t_inv, kb * jnp.exp(gc)[..., None])
    a_qk = jnp.where(causal, jnp.einsum('nbhid,nbhjd->nbhij', qc, kc) * decay, 0.0)

    def step(s, xs):
        qi, ki, ui, wi, gi, ai = xs
        v_new = ui - jnp.einsum('bhcd,bhde->bhce', wi, s)
        o = (jnp.einsum('bhcd,bhde->bhce', qi * jnp.exp(gi)[..., None], s)
             + jnp.einsum('bhij,bhje->bhie', ai, v_new))
        g_last = gi[..., -1:]
        s = (s * jnp.exp(g_last)[..., None]
             + jnp.einsum('bhcd,bhce->bhde', ki * jnp.exp(g_last - gi)[..., None], v_new))
        return s, o

    s_fin, o = lax.scan(step, s0.astype(F32), (qc, kc, u, w, gc, a_qk))
    return o.transpose(1, 0, 3, 2, 4).reshape(B, L, H, dv), s_fin


def deltanet_mixer(qkv, z, b_fb, a_fb, conv_w, a_log, dt_bias, norm_g, s0_f, s0_b):
    B, L, _ = qkv.shape
    qkv = jax.nn.silu(depthwise_conv(qkv, conv_w))
    q, k, v = jnp.split(qkv, 3, axis=-1)
    q = l2_normalize(q.reshape(B, L, DN_HEADS, DN_DK))
    k = l2_normalize(k.reshape(B, L, DN_HEADS, DN_DK))
    v = v.reshape(B, L, DN_HEADS, DN_DV)
    beta = jax.nn.sigmoid(b_fb.astype(F32)).reshape(B, L, 2, DN_HEADS)
    g = -jnp.exp(a_log.astype(F32)) * jax.nn.softplus(a_fb.astype(F32).reshape(B, L, 2, DN_HEADS) + dt_bias)
    o_f, s_f = gated_delta_chunked(q, k, v, g[:, :, 0], beta[:, :, 0], s0_f)
    flip = lambda t: jnp.flip(t, axis=1)
    o_b, s_b = gated_delta_chunked(flip(q), flip(k), flip(v), flip(g[:, :, 1]), flip(beta[:, :, 1]), s0_b)
    o = o_f + flip(o_b)
    o = rms_norm(o, norm_g) * jax.nn.silu(z.astype(F32).reshape(B, L, DN_HEADS, DN_DV))
    return o.reshape(B, L, GROUP_W).astype(qkv.dtype), s_f, s_b


def conv_module(u, dw_w, dw_b, gn_g, gn_b, pw_w, pw_b):
    a, gate = jnp.split(u, 2, axis=-1)
    h = depthwise_conv(a * jax.nn.sigmoid(gate), dw_w) + dw_b
    B, L, C = h.shape
    hf = h.astype(F32).reshape(B, L, CONV_GROUPS, C // CONV_GROUPS)
    mu = jnp.mean(hf, axis=-1, keepdims=True)
    var = jnp.mean(jnp.square(hf - mu), axis=-1, keepdims=True)
    hn = ((hf - mu) * lax.rsqrt(var + EPS)).reshape(B, L, C).astype(h.dtype) * gn_g + gn_b
    return jax.nn.silu(hn) @ pw_w + pw_b


def routed_experts(x, expert_id, gates, w1, w3, w2):
    n_tok, d = x.shape
    n_assign = n_tok * TOP_K
    flat = expert_id.reshape(-1)
    order = jnp.argsort(flat)
    sorted_e = flat[order]
    counts = jax.ops.segment_sum(jnp.ones((n_assign,), jnp.int32), flat, num_segments=N_EXPERTS)
    padded = (counts + EXPERT_BLOCK - 1) // EXPERT_BLOCK * EXPERT_BLOCK
    pad_end = jnp.cumsum(padded)
    pad_start = pad_end - padded
    start = jnp.cumsum(counts) - counts
    dest_sorted = pad_start[sorted_e] + jnp.arange(n_assign, dtype=jnp.int32) - start[sorted_e]
    n_blk = -(-n_assign // EXPERT_BLOCK) + N_EXPERTS
    slot_tok = jnp.full((n_blk * EXPERT_BLOCK,), n_tok, jnp.int32).at[dest_sorted].set(
        (order // TOP_K).astype(jnp.int32))
    blk_expert = jnp.minimum(jnp.searchsorted(pad_end, jnp.arange(n_blk) * EXPERT_BLOCK, side='right'),
                             N_EXPERTS - 1)
    x_pad = jnp.concatenate([x, jnp.zeros((1, d), x.dtype)], axis=0)
    xb = x_pad[slot_tok].reshape(n_blk, EXPERT_BLOCK, d)

    def expert_block(args):
        xe, e = args
        return (jax.nn.silu(xe @ w1[e]) * (xe @ w3[e])) @ w2[e]

    yb = lax.map(expert_block, (xb, blk_expert)).reshape(n_blk * EXPERT_BLOCK, d)
    dest = jnp.zeros((n_assign,), jnp.int32).at[order].set(dest_sorted)
    y = yb[dest].reshape(n_tok, TOP_K, d)
    return jnp.sum(y * gates[..., None].astype(y.dtype), axis=1)


def hier_moe(h, wg, bg, we, be, w1, w3, w2):
    B, L, D = h.shape
    n_tok = B * L
    x = h.reshape(n_tok, D)
    g_logits = (x @ wg).astype(F32) + bg.astype(F32)
    grp = jnp.argmax(g_logits, axis=-1)
    p_grp = jnp.take_along_axis(jax.nn.softmax(g_logits, axis=-1), grp[:, None], axis=-1)
    e_logits = ((x @ we).astype(F32) + be.astype(F32)).reshape(n_tok, N_GROUPS, E_PER_GROUP)
    e_logits = jnp.take_along_axis(e_logits, grp[:, None, None], axis=1)[:, 0]
    top_v, top_i = lax.top_k(e_logits, TOP_K)
    gates = jax.nn.softmax(top_v, axis=-1) * p_grp
    expert_id = (grp[:, None] * E_PER_GROUP + top_i).astype(jnp.int32)
    return routed_experts(x, expert_id, gates, w1, w3, w2).reshape(B, L, D)


def trunk_layer(x, cvec, lp, layer_idx, ctx):
    B, L, _ = x.shape
    mod = jax.nn.silu(cvec) @ lp['ada_w'] + lp['ada_b']
    sh1, sc1, g1, sh2, sc2, g2 = [m[:, None, :] for m in jnp.split(mod, 6, axis=-1)]
    h = rms_norm(x, lp['norm1_g']) * (1 + sc1) + sh1
    split_at = np.cumsum(IN_SIZES)[:-1].tolist()
    (da_q, da_k, da_v, cv_u, na_q, na_k, na_v,
     dn_qkv, dn_z, dn_b, dn_a) = jnp.split(h @ lp['w_in'], split_at, axis=-1)

    lam_init = 0.8 - 0.6 * math.exp(-0.3 * layer_idx)
    lam = (jnp.exp(jnp.sum(lp['da_lam_q1'].astype(F32) * lp['da_lam_k1'].astype(F32)))
           - jnp.exp(jnp.sum(lp['da_lam_q2'].astype(F32) * lp['da_lam_k2'].astype(F32))) + lam_init)
    qa = da_q.reshape(B, L, DA_HEADS, 2 * DA_DQ)
    ka = da_k.reshape(B, L, DA_HEADS, 2 * DA_DQ)
    va = da_v.reshape(B, L, DA_HEADS, DA_DV)
    if ctx is None:
        q_att, k_att, v_att = qa, ka, va
    else:
        q_att = rope_pairs(qa)
        k_att = jnp.concatenate([rope_pairs(ka), ctx[0]], axis=1)
        v_att = jnp.concatenate([va, ctx[1]], axis=1)
    o_a = diff_attention(q_att[..., :DA_DQ], q_att[..., DA_DQ:], k_att[..., :DA_DQ], k_att[..., DA_DQ:], v_att, lam)
    o_a = (rms_norm(o_a, lp['da_norm_g']) * (1.0 - lam_init)).reshape(B, L, GROUP_W)

    o_b = conv_module(cv_u, lp['cv_dw_w'], lp['cv_dw_b'], lp['cv_gn_g'], lp['cv_gn_b'], lp['cv_pw_w'], lp['cv_pw_b'])

    nq = na_q.reshape(B, L, NA_HEADS, NA_DH)
    nk = na_k.reshape(B, L, NA_HEADS, NA_DH)
    nv = na_v.reshape(B, L, NA_HEADS, NA_DH)
    if ctx is None:
        o_c = softmax_attention(nq, nk, nv)
    else:
        o_c = neighborhood_attention(nq, nk, nv, ctx[2], ctx[3], lp['na_rpb'])
    o_c = o_c.reshape(B, L, GROUP_W)

    if ctx is None:
        s0_f = jnp.zeros((B, DN_HEADS, DN_DK, DN_DV), F32)
        s0_b = jnp.zeros((B, DN_HEADS, DN_DK, DN_DV), F32)
    else:
        s0_f, s0_b = ctx[4], ctx[5]
    o_d, s_f, s_b = deltanet_mixer(dn_qkv, dn_z, dn_b, dn_a, lp['dn_conv_w'], lp['dn_a_log'],
                                   lp['dn_dt_bias'], lp['dn_norm_g'], s0_f, s0_b)

    mixed = jnp.concatenate([o_a, o_b, o_c, o_d], axis=-1)
    x = x + g1 * (mixed @ lp['w_out'])
    h2 = rms_norm(x, lp['norm2_g']) * (1 + sc2) + sh2
    x = x + g2 * hier_moe(h2, lp['moe_wg'], lp['moe_bg'], lp['moe_we'], lp['moe_be'],
                          lp['moe_w1'], lp['moe_w3'], lp['moe_w2'])
    return x, (ka, va, nk, nv, s_f, s_b)


def setup_inputs(seed: int = 0) -> dict:
    key = jax.random.key(seed)
    keys = iter(jax.random.split(key, 64))
    D = D_MODEL

    def nrm(shape, scale):
        return jax.random.normal(next(keys), shape, jnp.float32) * scale

    return {
        'x_prompt': nrm((BATCH, SEQ, D), 1.0),
        'x_sample': nrm((DEC_BATCH, DEC_SEQ, D), 1.0),
        'cache_da_k': nrm((DEC_BATCH, DEPTH, PAST_LEN, DA_HEADS, 2 * DA_DQ), 1.0),
        'cache_da_v': nrm((DEC_BATCH, DEPTH, PAST_LEN, DA_HEADS, DA_DV), 1.0),
        'cache_na_k': nrm((DEC_BATCH, DEPTH, PAST_LEN, NA_HEADS, NA_DH), 1.0),
        'cache_na_v': nrm((DEC_BATCH, DEPTH, PAST_LEN, NA_HEADS, NA_DH), 1.0),
        'state_dn_fwd': nrm((DEC_BATCH, DEPTH, DN_HEADS, DN_DK, DN_DV), 0.3),
        'state_dn_bwd': nrm((DEC_BATCH, DEPTH, DN_HEADS, DN_DK, DN_DV), 0.3),
        'c': nrm((DEC_BATCH, D), 1.0),
        'c_ctx': nrm((D,), 1.0),
        'w_in': nrm((DEPTH, D, N_IN), D ** -0.5),
        'w_out': nrm((DEPTH, D, D), D ** -0.5),
        'ada_w': nrm((DEPTH, D, 6 * D), 0.5 * D ** -0.5),
        'ada_b': nrm((DEPTH, 6 * D), 0.02),
        'norm1_g': 1.0 + nrm((DEPTH, D), 0.02),
        'norm2_g': 1.0 + nrm((DEPTH, D), 0.02),
        'da_lam_q1': nrm((DEPTH, DA_DQ), 0.1),
        'da_lam_k1': nrm((DEPTH, DA_DQ), 0.1),
        'da_lam_q2': nrm((DEPTH, DA_DQ), 0.1),
        'da_lam_k2': nrm((DEPTH, DA_DQ), 0.1),
        'da_norm_g': 1.0 + nrm((DEPTH, DA_DV), 0.02),
        'cv_dw_w': nrm((DEPTH, CONV_W, GROUP_W), CONV_W ** -0.5),
        'cv_dw_b': nrm((DEPTH, GROUP_W), 0.02),
        'cv_gn_g': 1.0 + nrm((DEPTH, GROUP_W), 0.02),
        'cv_gn_b': nrm((DEPTH, GROUP_W), 0.02),
        'cv_pw_w': nrm((DEPTH, GROUP_W, GROUP_W), GROUP_W ** -0.5),
        'cv_pw_b': nrm((DEPTH, GROUP_W), 0.02),
        'na_rpb': nrm((DEPTH, NA_HEADS, 2 * WIN_R - 1, 2 * WIN_C - 1), 0.1),
        'dn_conv_w': nrm((DEPTH, DN_CONV_W, 3 * GROUP_W), DN_CONV_W ** -0.5),
        'dn_a_log': jnp.log(jax.random.uniform(next(keys), (DEPTH, 2, DN_HEADS), jnp.float32, 1.0, 16.0)),
        'dn_dt_bias': -2.0 + nrm((DEPTH, 2, DN_HEADS), 0.5),
        'dn_norm_g': 1.0 + nrm((DEPTH, DN_DV), 0.02),
        'moe_wg': nrm((DEPTH, D, N_GROUPS), D ** -0.5),
        'moe_bg': nrm((DEPTH, N_GROUPS), 0.01),
        'moe_we': nrm((DEPTH, D, N_EXPERTS), D ** -0.5),
        'moe_be': nrm((DEPTH, N_EXPERTS), 0.01),
        'moe_w1': nrm((DEPTH, N_EXPERTS, D, D_FF_EXPERT), D ** -0.5),
        'moe_w3': nrm((DEPTH, N_EXPERTS, D, D_FF_EXPERT), D ** -0.5),
        'moe_w2': nrm((DEPTH, N_EXPERTS, D_FF_EXPERT, D), D_FF_EXPERT ** -0.5),
        'final_norm_g': 1.0 + nrm((D,), 0.02),
    }


def reference(x_prompt, x_sample, cache_da_k, cache_da_v, cache_na_k, cache_na_v, state_dn_fwd, state_dn_bwd,
              c, c_ctx, w_in, w_out, ada_w, ada_b, norm1_g, norm2_g, da_lam_q1, da_lam_k1, da_lam_q2, da_lam_k2,
              da_norm_g, cv_dw_w, cv_dw_b, cv_gn_g, cv_gn_b, cv_pw_w, cv_pw_b, na_rpb, dn_conv_w, dn_a_log,
              dn_dt_bias, dn_norm_g, moe_wg, moe_bg, moe_we, moe_be, moe_w1, moe_w3, moe_w2, final_norm_g):
    xp = x_prompt
    xs = x_sample
    new_da_k, new_da_v, new_na_k, new_na_v, new_sf, new_sb = [], [], [], [], [], []
    for l in range(DEPTH):
        lp = {
            'w_in': w_in[l], 'w_out': w_out[l], 'ada_w': ada_w[l], 'ada_b': ada_b[l],
            'norm1_g': norm1_g[l], 'norm2_g': norm2_g[l],
            'da_lam_q1': da_lam_q1[l], 'da_lam_k1': da_lam_k1[l], 'da_lam_q2': da_lam_q2[l],
            'da_lam_k2': da_lam_k2[l], 'da_norm_g': da_norm_g[l],
            'cv_dw_w': cv_dw_w[l], 'cv_dw_b': cv_dw_b[l], 'cv_gn_g': cv_gn_g[l], 'cv_gn_b': cv_gn_b[l],
            'cv_pw_w': cv_pw_w[l], 'cv_pw_b': cv_pw_b[l], 'na_rpb': na_rpb[l],
            'dn_conv_w': dn_conv_w[l], 'dn_a_log': dn_a_log[l], 'dn_dt_bias': dn_dt_bias[l],
            'dn_norm_g': dn_norm_g[l],
            'moe_wg': moe_wg[l], 'moe_bg': moe_bg[l], 'moe_we': moe_we[l], 'moe_be': moe_be[l],
            'moe_w1': moe_w1[l], 'moe_w3': moe_w3[l], 'moe_w2': moe_w2[l],
        }
        xp, ctx_new = trunk_layer(xp, c_ctx[None, :], lp, l, None)
        new_da_k.append(ctx_new[0])
        new_da_v.append(ctx_new[1])
        new_na_k.append(ctx_new[2])
        new_na_v.append(ctx_new[3])
        new_sf.append(ctx_new[4])
        new_sb.append(ctx_new[5])
        ctx_cached = (cache_da_k[:, l], cache_da_v[:, l], cache_na_k[:, l], cache_na_v[:, l],
                      state_dn_fwd[:, l], state_dn_bwd[:, l])
        xs, _ = trunk_layer(xs, c, lp, l, ctx_cached)
    y_prompt = rms_norm(xp, final_norm_g)
    y_sample = rms_norm(xs, final_norm_g)
    return (y_prompt, y_sample,
            jnp.stack(new_da_k, axis=1), jnp.stack(new_da_v, axis=1),
            jnp.stack(new_na_k, axis=1), jnp.stack(new_na_v, axis=1),
            jnp.stack(new_sf, axis=1), jnp.stack(new_sb, axis=1))
```

```python
import functools
import math

import numpy as np
import jax
import jax.numpy as jnp
from jax import lax
from jax.experimental import pallas as pl
from jax.experimental.pallas import tpu as pltpu

F32 = jnp.float32
BF16 = jnp.bfloat16

D_MODEL = 2048
BATCH = 16
SEQ = 256
DEPTH = 4
DEC_BATCH = 8
DEC_SEQ = 1024
PAST_LEN = 512
GRID_W = 64
GROUP_W = 512
DA_HEADS = 4
DA_DV = 128
DA_DQ = 64
ROPE_BASE = 10000.0
CONV_W = 31
CONV_GROUPS = 4
NA_HEADS = 4
NA_DH = 128
WIN_R = 8
WIN_C = 16
NA_QCOLS = 16
NA_KCOLS = 2 * WIN_C
DN_HEADS = 4
DN_DK = 128
DN_DV = 128
DN_CONV_W = 3
CHUNK = 64
N_GROUPS = 4
E_PER_GROUP = 8
N_EXPERTS = 32
TOP_K = 2
D_FF_EXPERT = 512
EPS = 1e-6
NEG_INF = -1e30
N_MAIN = 12 * GROUP_W
N_SMALL = 4 * DN_HEADS

M_PROMPT = BATCH * SEQ
M_SAMPLE = DEC_BATCH * DEC_SEQ
M_ALL = M_PROMPT + M_SAMPLE
N_SEG = 1 + DEC_BATCH

LANES = 128
VMEM_LIMIT = 56 * 1024 * 1024

COL_DA_Q, COL_DA_K, COL_DA_V = 0, 4, 8
COL_CV_U = 12
COL_NA_Q, COL_NA_K, COL_NA_V = 20, 24, 28
COL_DN_QKV, COL_DN_Z = 32, 44


def _cparams(sem):
    return pltpu.CompilerParams(dimension_semantics=sem, vmem_limit_bytes=VMEM_LIMIT)


def _seg_of_tile(i, tm):
    n_prompt_tiles = M_PROMPT // tm
    per_batch = DEC_SEQ // tm
    return jnp.where(i < n_prompt_tiles, 0, 1 + (i - n_prompt_tiles) // per_batch)


ADA_TN = 1536
ADA_KS = 256


def _ada_kernel(c_ref, w_ref, b_ref, o_ref):
    c = c_ref[...]
    s = (c * jax.nn.sigmoid(c)).astype(BF16)
    acc = jnp.zeros((16, ADA_TN), F32)
    for k in range(D_MODEL // ADA_KS):
        w = w_ref[k * ADA_KS:(k + 1) * ADA_KS, :].astype(BF16)
        acc = acc + jnp.dot(s[:, k * ADA_KS:(k + 1) * ADA_KS], w, preferred_element_type=F32)
    o_ref[...] = acc + b_ref[...]


def _ada_all(cvec16, ada_w, ada_b):
    n = 6 * D_MODEL
    return pl.pallas_call(
        _ada_kernel,
        out_shape=jax.ShapeDtypeStruct((DEPTH, 16, n), F32),
        grid=(DEPTH, n // ADA_TN),
        in_specs=[pl.BlockSpec((16, D_MODEL), lambda l, j: (0, 0)),
                  pl.BlockSpec((None, D_MODEL, ADA_TN), lambda l, j: (l, 0, j)),
                  pl.BlockSpec((None, 1, ADA_TN), lambda l, j: (l, 0, j))],
        out_specs=pl.BlockSpec((None, 16, ADA_TN), lambda l, j: (l, 0, j)),
        compiler_params=_cparams(("arbitrary", "arbitrary")),
        name="ada_mod",
    )(cvec16, ada_w, ada_b.reshape(DEPTH, 1, n))


K1_TM = 1024
K1_TN = 512
ROW_CHUNK = 128


def _modulated_norm(x, g, sc, sh):
    ms = jnp.mean(x * x, axis=-1, keepdims=True)
    return (x * lax.rsqrt(ms + EPS)) * g * (1.0 + sc) + sh


def _k1_kernel(x_ref, g_ref, mod_ref, w_ref, ws_ref, o_ref, os_ref, h_scr):
    j = pl.program_id(1)

    @pl.when(j == 0)
    def _():
        g = g_ref[...]
        sh = mod_ref[0:1, :]
        sc = mod_ref[1:2, :]

        def body(r, carry):
            rows = pl.ds(pl.multiple_of(r * ROW_CHUNK, ROW_CHUNK), ROW_CHUNK)
            h = _modulated_norm(x_ref[rows, :], g, sc, sh).astype(BF16)
            h_scr[rows, :] = h
            os_ref[rows, :] = jnp.dot(h, ws_ref[...], preferred_element_type=F32)
            return carry

        lax.fori_loop(0, K1_TM // ROW_CHUNK, body, 0)

    o_ref[...] = jnp.dot(h_scr[...], w_ref[...], preferred_element_type=F32)


def _input_projection(x_all, norm_g, mod, w_main, w_small):
    return pl.pallas_call(
        _k1_kernel,
        out_shape=(jax.ShapeDtypeStruct((M_ALL, N_MAIN), F32),
                   jax.ShapeDtypeStruct((M_ALL, LANES), F32)),
        grid=(M_ALL // K1_TM, N_MAIN // K1_TN),
        in_specs=[pl.BlockSpec((K1_TM, D_MODEL), lambda i, j: (i, 0)),
                  pl.BlockSpec((1, D_MODEL), lambda i, j: (0, 0)),
                  pl.BlockSpec((None, 6, D_MODEL), lambda i, j: (_seg_of_tile(i, K1_TM), 0, 0)),
                  pl.BlockSpec((D_MODEL, K1_TN), lambda i, j: (0, j)),
                  pl.BlockSpec((D_MODEL, LANES), lambda i, j: (0, 0))],
        out_specs=(pl.BlockSpec((K1_TM, K1_TN), lambda i, j: (i, j)),
                   pl.BlockSpec((K1_TM, LANES), lambda i, j: (i, 0))),
        scratch_shapes=[pltpu.VMEM((K1_TM, D_MODEL), BF16)],
        compiler_params=_cparams(("arbitrary", "arbitrary")),
        name="input_proj",
    )(x_all, norm_g.reshape(1, D_MODEL), mod, w_main, w_small)


DA_QB = 256


def _rope_tables():
    t = jnp.arange(DEC_SEQ)
    nf = DA_DQ // 4
    freqs = ROPE_BASE ** (-jnp.arange(nf, dtype=F32) / nf)
    lane = np.arange(LANES)
    m = lane % DA_DQ
    use_col = (m // (DA_DQ // 2)) == 1
    within = m % (DA_DQ // 2)
    fi = within % nf
    second = within >= nf
    pos = jnp.where(jnp.asarray(use_col)[None, :], (t % GRID_W)[:, None], (t // GRID_W)[:, None]).astype(F32)
    ang = pos * freqs[jnp.asarray(fi)][None, :]
    cos = jnp.cos(ang)
    sin = jnp.sin(ang)
    sec = jnp.asarray(second)[None, :]
    sin_next = jnp.where(sec, 0.0, -sin)
    sin_prev = jnp.where(sec, sin, 0.0)
    return cos.astype(F32), sin_next.astype(F32), sin_prev.astype(F32)


def _rope(x, cos, sin_next, sin_prev):
    nf = DA_DQ // 4
    return (x * cos + pltpu.roll(x, LANES - nf, 1) * sin_next + pltpu.roll(x, nf, 1) * sin_prev)


def _softmax_rows(s):
    m = jnp.max(s, axis=-1, keepdims=True)
    e = jnp.exp(s - m)
    return e / jnp.sum(e, axis=-1, keepdims=True)


_NT = (((1,), (1,)), ((), ()))


def _da_kernel(*refs, seq, ctx_len, lam_init):
    if ctx_len:
        (q_ref, k_ref, v_ref, kc_ref, vc_ref, cos_ref, sn_ref, sp_ref, lam_ref, g_ref, _, o_ref,
         k_scr, v_scr) = refs
    else:
        q_ref, k_ref, v_ref, lam_ref, g_ref, o_ref, k_scr, v_scr = refs
    lam_vec = lam_ref[...]
    lam = (jnp.exp(jnp.sum(lam_vec[0:1, :] * lam_vec[1:2, :], axis=-1, keepdims=True))
           - jnp.exp(jnp.sum(lam_vec[2:3, :] * lam_vec[3:4, :], axis=-1, keepdims=True)) + lam_init)
    if ctx_len:
        k_scr[0:seq, :] = _rope(k_ref[...], cos_ref[...], sn_ref[...], sp_ref[...]).astype(BF16)
        k_scr[seq:seq + ctx_len, :] = kc_ref[...].astype(BF16)
        v_scr[0:seq, :] = v_ref[...].astype(BF16)
        v_scr[seq:seq + ctx_len, :] = vc_ref[...].astype(BF16)
    else:
        k_scr[...] = k_ref[...].astype(BF16)
        v_scr[...] = v_ref[...].astype(BF16)
    lane = lax.broadcasted_iota(jnp.int32, (DA_QB, LANES), 1)
    first_map = lane < DA_DQ
    scale = DA_DQ ** -0.5
    gain = g_ref[...] * (1.0 - lam_init)

    def body(qi, carry):
        rows = pl.ds(pl.multiple_of(qi * DA_QB, DA_QB), DA_QB)
        q = q_ref[rows, :]
        if ctx_len:
            q = _rope(q, cos_ref[rows, :], sn_ref[rows, :], sp_ref[rows, :])
        q = q * scale
        q1 = jnp.where(first_map, q, 0.0).astype(BF16)
        q2 = jnp.where(first_map, 0.0, q).astype(BF16)
        kk = k_scr[...]
        p1 = _softmax_rows(lax.dot_general(q1, kk, _NT, preferred_element_type=F32))
        p2 = _softmax_rows(lax.dot_general(q2, kk, _NT, preferred_element_type=F32))
        p = (p1 - lam * p2).astype(BF16)
        o = jnp.dot(p, v_scr[...], preferred_element_type=F32)
        ms = jnp.mean(o * o, axis=-1, keepdims=True)
        o_ref[rows, :] = (o * lax.rsqrt(ms + EPS)) * gain
        return carry

    lax.fori_loop(0, seq // DA_QB, body, 0)


def _diff_attention_prompt(proj, lam_vec, norm_g, lam_init):
    kern = functools.partial(_da_kernel, seq=SEQ, ctx_len=0, lam_init=lam_init)
    blk = lambda c0: pl.BlockSpec((SEQ, LANES), lambda b, h: (b, c0 + h))
    return pl.pallas_call(
        kern,
        out_shape=jax.ShapeDtypeStruct((M_ALL, GROUP_W), F32),
        grid=(BATCH, DA_HEADS),
        in_specs=[blk(COL_DA_Q), blk(COL_DA_K), blk(COL_DA_V),
                  pl.BlockSpec((4, DA_DQ), lambda b, h: (0, 0)),
                  pl.BlockSpec((1, DA_DV), lambda b, h: (0, 0))],
        out_specs=pl.BlockSpec((SEQ, LANES), lambda b, h: (b, h)),
        scratch_shapes=[pltpu.VMEM((SEQ, LANES), BF16), pltpu.VMEM((SEQ, LANES), BF16)],
        compiler_params=_cparams(("arbitrary", "arbitrary")),
        name="diff_attn_prompt",
    )(proj, proj, proj, lam_vec, norm_g.reshape(1, DA_DV))


def _diff_attention_sample(proj, cache_k, cache_v, layer, rope_tabs, lam_vec, norm_g, lam_init, out_buf):
    kern = functools.partial(_da_kernel, seq=DEC_SEQ, ctx_len=PAST_LEN, lam_init=lam_init)
    off = M_PROMPT // DEC_SEQ
    blk = lambda c0: pl.BlockSpec((DEC_SEQ, LANES), lambda b, h: (off + b, c0 + h))
    cblk = pl.BlockSpec((None, None, PAST_LEN, LANES), lambda b, h: (b, layer, 0, h))
    tab = pl.BlockSpec((DEC_SEQ, LANES), lambda b, h: (0, 0))
    cos, sn, sp = rope_tabs
    return pl.pallas_call(
        kern,
        out_shape=jax.ShapeDtypeStruct((M_ALL, GROUP_W), F32),
        grid=(DEC_BATCH, DA_HEADS),
        in_specs=[blk(COL_DA_Q), blk(COL_DA_K), blk(COL_DA_V), cblk, cblk, tab, tab, tab,
                  pl.BlockSpec((4, DA_DQ), lambda b, h: (0, 0)),
                  pl.BlockSpec((1, DA_DV), lambda b, h: (0, 0)),
                  pl.BlockSpec(memory_space=pl.ANY)],
        out_specs=pl.BlockSpec((DEC_SEQ, LANES), lambda b, h: (off + b, h)),
        scratch_shapes=[pltpu.VMEM((DEC_SEQ + PAST_LEN, LANES), BF16),
                        pltpu.VMEM((DEC_SEQ + PAST_LEN, LANES), BF16)],
        input_output_aliases={10: 0},
        compiler_params=_cparams(("arbitrary", "arbitrary")),
        name="diff_attn_sample",
    )(proj, proj, proj, cache_k, cache_v, cos, sn, sp, lam_vec, norm_g.reshape(1, DA_DV), out_buf)


def _sm_attn_kernel(q_ref, k_ref, v_ref, o_ref):
    q = (q_ref[...] * (NA_DH ** -0.5)).astype(BF16)
    p = _softmax_rows(lax.dot_general(q, k_ref[...].astype(BF16), _NT, preferred_element_type=F32))
    o_ref[...] = jnp.dot(p.astype(BF16), v_ref[...].astype(BF16), preferred_element_type=F32)


def _softmax_attention_prompt(proj):
    blk = lambda c0: pl.BlockSpec((SEQ, LANES), lambda b, h: (b, c0 + h))
    return pl.pallas_call(
        _sm_attn_kernel,
        out_shape=jax.ShapeDtypeStruct((M_ALL, GROUP_W), F32),
        grid=(BATCH, NA_HEADS),
        in_specs=[blk(COL_NA_Q), blk(COL_NA_K), blk(COL_NA_V)],
        out_specs=pl.BlockSpec((SEQ, LANES), lambda b, h: (b, h)),
        compiler_params=_cparams(("arbitrary", "arbitrary")),
        name="softmax_attn_prompt",
    )(proj, proj, proj)


K2_TM = 512


def _split_bf16(a):
    hi = a.astype(BF16)
    lo = (a - hi.astype(F32)).astype(BF16)
    return hi, lo


def _k2_kernel(oa_ref, ob_ref, oc_ref, od_ref, x_ref, w_ref, mod_ref, g_ref, wr_hi_ref, wr_lo_ref, br_ref,
               xo_ref, h2_ref, lg_ref):
    acc = jnp.dot(oa_ref[...].astype(BF16), w_ref[0 * GROUP_W:1 * GROUP_W, :], preferred_element_type=F32)
    acc += jnp.dot(ob_ref[...].astype(BF16), w_ref[1 * GROUP_W:2 * GROUP_W, :], preferred_element_type=F32)
    acc += jnp.dot(oc_ref[...].astype(BF16), w_ref[2 * GROUP_W:3 * GROUP_W, :], preferred_element_type=F32)
    acc += jnp.dot(od_ref[...].astype(BF16), w_ref[3 * GROUP_W:4 * GROUP_W, :], preferred_element_type=F32)
    x = x_ref[...] + mod_ref[2:3, :] * acc
    xo_ref[...] = x
    h2 = _modulated_norm(x, g_ref[...], mod_ref[4:5, :], mod_ref[3:4, :])
    hi, lo = _split_bf16(h2)
    h2_ref[...] = hi
    lg = (jnp.dot(hi, wr_hi_ref[...], preferred_element_type=F32)
          + jnp.dot(hi, wr_lo_ref[...], preferred_element_type=F32)
          + jnp.dot(lo, wr_hi_ref[...], preferred_element_type=F32))
    lg_ref[...] = lg + br_ref[...]


def _output_projection(oa, ob, oc, od, x_all, w_out_bf16, mod, norm2_g, wr_hi, wr_lo, br):
    mix = pl.BlockSpec((K2_TM, GROUP_W), lambda i: (i, 0))
    row = pl.BlockSpec((K2_TM, D_MODEL), lambda i: (i, 0))
    const = lambda shape: pl.BlockSpec(shape, lambda i: (0,) * len(shape))
    return pl.pallas_call(
        _k2_kernel,
        out_shape=(jax.ShapeDtypeStruct((M_ALL, D_MODEL), F32),
                   jax.ShapeDtypeStruct((M_ALL, D_MODEL), BF16),
                   jax.ShapeDtypeStruct((M_ALL, LANES), F32)),
        grid=(M_ALL // K2_TM,),
        in_specs=[mix, mix, mix, mix, row, const((D_MODEL, D_MODEL)),
                  pl.BlockSpec((None, 6, D_MODEL), lambda i: (_seg_of_tile(i, K2_TM), 0, 0)),
                  const((1, D_MODEL)), const((D_MODEL, LANES)), const((D_MODEL, LANES)), const((1, LANES))],
        out_specs=(row, row, pl.BlockSpec((K2_TM, LANES), lambda i: (i, 0))),
        compiler_params=_cparams(("arbitrary",)),
        name="output_proj",
    )(oa, ob, oc, od, x_all, w_out_bf16, mod, norm2_g.reshape(1, D_MODEL), wr_hi, wr_lo, br)


MOE_TB = 256
N_ASSIGN = M_ALL * TOP_K
MOE_NBLK = N_ASSIGN // MOE_TB + N_EXPERTS


def _moe_kernel(be_ref, nused_ref, x_ref, w1_ref, w3_ref, w2_ref, y_ref, w1_scr, w3_scr, w2_scr):
    i = pl.program_id(0)
    prev = be_ref[jnp.maximum(i - 1, 0)]
    new_expert = jnp.logical_or(i == 0, be_ref[i] != prev)

    @pl.when(new_expert)
    def _():
        w1_scr[...] = w1_ref[...].astype(BF16)
        w3_scr[...] = w3_ref[...].astype(BF16)
        w2_scr[...] = w2_ref[...].astype(BF16)

    @pl.when(i < nused_ref[0])
    def _():
        x = x_ref[...]
        a = jnp.dot(x, w1_scr[...], preferred_element_type=F32)
        b = jnp.dot(x, w3_scr[...], preferred_element_type=F32)
        hmid = (a * jax.nn.sigmoid(a) * b).astype(BF16)
        y_ref[...] = jnp.dot(hmid, w2_scr[...], preferred_element_type=F32)

    @pl.when(i >= nused_ref[0])
    def _():
        y_ref[...] = jnp.zeros_like(y_ref)


def _moe_experts(blk_expert, n_used, xb, w1, w3, w2, layer):
    grid_spec = pltpu.PrefetchScalarGridSpec(
        num_scalar_prefetch=2,
        grid=(MOE_NBLK,),
        in_specs=[pl.BlockSpec((MOE_TB, D_MODEL), lambda i, be, nu: (i, 0)),
                  pl.BlockSpec((None, None, D_MODEL, D_FF_EXPERT), lambda i, be, nu: (layer, be[i], 0, 0)),
                  pl.BlockSpec((None, None, D_MODEL, D_FF_EXPERT), lambda i, be, nu: (layer, be[i], 0, 0)),
                  pl.BlockSpec((None, None, D_FF_EXPERT, D_MODEL), lambda i, be, nu: (layer, be[i], 0, 0))],
        out_specs=pl.BlockSpec((MOE_TB, D_MODEL), lambda i, be, nu: (i, 0)),
        scratch_shapes=[pltpu.VMEM((D_MODEL, D_FF_EXPERT), BF16),
                        pltpu.VMEM((D_MODEL, D_FF_EXPERT), BF16),
                        pltpu.VMEM((D_FF_EXPERT, D_MODEL), BF16)])
    return pl.pallas_call(
        _moe_kernel,
        out_shape=jax.ShapeDtypeStruct((MOE_NBLK * MOE_TB, D_MODEL), F32),
        grid_spec=grid_spec,
        compiler_params=_cparams(("arbitrary",)),
        name="moe_experts",
    )(blk_expert, n_used, xb, w1, w3, w2)


def _route(logits):
    g_logits = logits[:, :N_GROUPS]
    grp = jnp.argmax(g_logits, axis=-1)
    p_grp = jnp.take_along_axis(jax.nn.softmax(g_logits, axis=-1), grp[:, None], axis=-1)
    e_logits = logits[:, N_GROUPS:N_GROUPS + N_EXPERTS].reshape(M_ALL, N_GROUPS, E_PER_GROUP)
    e_logits = jnp.take_along_axis(e_logits, grp[:, None, None], axis=1)[:, 0]
    top_v, top_i = lax.top_k(e_logits, TOP_K)
    gates = jax.nn.softmax(top_v, axis=-1) * p_grp
    expert_id = (grp[:, None] * E_PER_GROUP + top_i).astype(jnp.int32)
    return expert_id, gates


def _dispatch_plan(expert_id):
    flat = expert_id.reshape(-1)
    order = jnp.argsort(flat)
    sorted_e = flat[order]
    counts = jnp.zeros((N_EXPERTS,), jnp.int32).at[flat].add(1)
    padded = (counts + MOE_TB - 1) // MOE_TB * MOE_TB
    pad_end = jnp.cumsum(padded)
    pad_start = pad_end - padded
    start = jnp.cumsum(counts) - counts
    dest_sorted = pad_start[sorted_e] + jnp.arange(N_ASSIGN, dtype=jnp.int32) - start[sorted_e]
    slot_tok = jnp.zeros((MOE_NBLK * MOE_TB,), jnp.int32).at[dest_sorted].set((order // TOP_K).astype(jnp.int32))
    n_used = (pad_end[-1] // MOE_TB).astype(jnp.int32)
    blk_start = jnp.minimum(jnp.arange(MOE_NBLK, dtype=jnp.int32), n_used - 1) * MOE_TB
    blk_expert = jnp.minimum(jnp.searchsorted(pad_end, blk_start, side='right'), N_EXPERTS - 1).astype(jnp.int32)
    dest = jnp.zeros((N_ASSIGN,), jnp.int32).at[order].set(dest_sorted)
    return slot_tok, blk_expert, n_used.reshape(1), dest


CB_TM = 512


def _combine_kernel(x_ref, y_ref, gate_ref, mod_ref, fg_ref, xo_ref, *maybe_norm_ref):
    gates = gate_ref[...]
    y = y_ref[:, 0:D_MODEL] * gates[:, 0:1] + y_ref[:, D_MODEL:2 * D_MODEL] * gates[:, 1:2]
    x = x_ref[...] + mod_ref[5:6, :] * y
    xo_ref[...] = x
    if maybe_norm_ref:
        ms = jnp.mean(x * x, axis=-1, keepdims=True)
        maybe_norm_ref[0][...] = (x * lax.rsqrt(ms + EPS)) * fg_ref[...]


def _combine(x_all, y_pair, gates, mod, final_g, with_final_norm):
    row = pl.BlockSpec((CB_TM, D_MODEL), lambda i: (i, 0))
    out_shape = [jax.ShapeDtypeStruct((M_ALL, D_MODEL), F32)]
    out_specs = [row]
    if with_final_norm:
        out_shape.append(jax.ShapeDtypeStruct((M_ALL, D_MODEL), F32))
        out_specs.append(row)
    return pl.pallas_call(
        _combine_kernel,
        out_shape=tuple(out_shape),
        grid=(M_ALL // CB_TM,),
        in_specs=[row, pl.BlockSpec((CB_TM, 2 * D_MODEL), lambda i: (i, 0)),
                  pl.BlockSpec((CB_TM, TOP_K), lambda i: (i, 0)),
                  pl.BlockSpec((None, 6, D_MODEL), lambda i: (_seg_of_tile(i, CB_TM), 0, 0)),
                  pl.BlockSpec((1, D_MODEL), lambda i: (0, 0))],
        out_specs=tuple(out_specs),
        compiler_params=_cparams(("arbitrary",)),
        name="moe_combine_final" if with_final_norm else "moe_combine",
    )(x_all, y_pair, gates, mod, final_g.reshape(1, D_MODEL))


def _rms_norm(x, g):
    xf = x.astype(F32)
    y = xf * lax.rsqrt(jnp.mean(xf * xf, axis=-1, keepdims=True) + EPS)
    return y.astype(x.dtype) * g


def _l2_normalize(x):
    xf = x.astype(F32)
    return xf * lax.rsqrt(jnp.sum(xf * xf, axis=-1, keepdims=True) + EPS)


def _depthwise_conv(x, w):
    k = w.shape[0]
    return lax.conv_general_dilated(x, w[:, None, :], (1,), ((k // 2, k // 2),),
                                    dimension_numbers=('NWC', 'WIO', 'NWC'),
                                    feature_group_count=x.shape[-1])


def _conv_module(u, dw_w, dw_b, gn_g, gn_b, pw_w, pw_b):
    a, gate = jnp.split(u, 2, axis=-1)
    h = _depthwise_conv(a * jax.nn.sigmoid(gate), dw_w) + dw_b
    B, L, C = h.shape
    hf = h.astype(F32).reshape(B, L, CONV_GROUPS, C // CONV_GROUPS)
    mu = jnp.mean(hf, axis=-1, keepdims=True)
    var = jnp.mean(jnp.square(hf - mu), axis=-1, keepdims=True)
    hn = ((hf - mu) * lax.rsqrt(var + EPS)).reshape(B, L, C).astype(h.dtype) * gn_g + gn_b
    return jax.nn.silu(hn) @ pw_w + pw_b


def _na_column_blocks():
    n_cb = GRID_W // NA_QCOLS
    cols = np.arange(GRID_W).reshape(n_cb, NA_QCOLS)
    win_start = np.clip(cols - WIN_C // 2, 0, GRID_W - WIN_C)
    key_start = np.minimum(win_start[:, 0], GRID_W - NA_KCOLS)
    key_cols = key_start[:, None] + np.arange(NA_KCOLS)
    mask = ((key_cols[:, None, :] >= win_start[:, :, None])
            & (key_cols[:, None, :] < win_start[:, :, None] + WIN_C))
    col_off = np.clip(key_cols[:, None, :] - cols[:, :, None] + WIN_C - 1, 0, 2 * WIN_C - 2)
    return key_cols, mask, col_off


def _neighborhood_attention(q, k, v, k_ctx, v_ctx, rpb):
    B, L, H, dh = q.shape
    rows = L // GRID_W
    kh = min(WIN_R, rows)
    n_cb = GRID_W // NA_QCOLS
    key_cols, mask, col_off = _na_column_blocks()
    key_cols = jnp.asarray(key_cols)
    mask = jnp.asarray(mask)[None, None, :, :, None, :]
    col_off = jnp.asarray(col_off)
    scale = dh ** -0.5
    qg = q.reshape(B, rows, GRID_W, H, dh)
    kg = k.reshape(B, rows, GRID_W, H, dh)
    vg = v.reshape(B, rows, GRID_W, H, dh)
    n_win = kh * NA_KCOLS

    def one_row(r):
        rs = jnp.clip(r - kh // 2, 0, rows - kh)
        kb = lax.dynamic_slice_in_dim(kg, rs, kh, axis=1)[:, :, key_cols]
        vb = lax.dynamic_slice_in_dim(vg, rs, kh, axis=1)[:, :, key_cols]
        qr = lax.dynamic_index_in_dim(qg, r, axis=1, keepdims=False).reshape(B, n_cb, NA_QCOLS, H, dh)
        row_idx = rs + jnp.arange(kh) - r + WIN_R - 1
        bias = rpb[:, row_idx][:, :, col_off].transpose(0, 2, 3, 1, 4)
        s_win = jnp.einsum('bjqhd,bajkhd->bhjqak', qr, kb).astype(F32) * scale + bias.astype(F32)
        s_win = jnp.where(mask, s_win, NEG_INF).reshape(B, H, n_cb, NA_QCOLS, n_win)
        s_ctx = jnp.einsum('bjqhd,bkhd->bhjqk', qr, k_ctx).astype(F32) * scale
        p = jax.nn.softmax(jnp.concatenate([s_win, s_ctx], axis=-1), axis=-1).astype(v.dtype)
        p_win = p[..., :n_win].reshape(B, H, n_cb, NA_QCOLS, kh, NA_KCOLS)
        o = (jnp.einsum('bhjqak,bajkhd->bjqhd', p_win, vb)
             + jnp.einsum('bhjqk,bkhd->bjqhd', p[..., n_win:], v_ctx))
        return o.reshape(B, GRID_W, H, dh)

    o = lax.map(one_row, jnp.arange(rows))
    return o.transpose(1, 0, 2, 3, 4).reshape(B, L, H, dh)


def _gated_delta_chunked(q, k, v, g, beta, s0):
    B, L, H, dk = q.shape
    dv = v.shape[-1]
    n = L // CHUNK
    qc = (q.astype(F32) * dk ** -0.5).reshape(B, n, CHUNK, H, dk).transpose(1, 0, 3, 2, 4)
    kc = k.astype(F32).reshape(B, n, CHUNK, H, dk).transpose(1, 0, 3, 2, 4)
    vc = v.astype(F32).reshape(B, n, CHUNK, H, dv).transpose(1, 0, 3, 2, 4)
    gc = jnp.cumsum(g.astype(F32).reshape(B, n, CHUNK, H).transpose(1, 0, 3, 2), axis=-1)
    bc = beta.astype(F32).reshape(B, n, CHUNK, H).transpose(1, 0, 3, 2)
    causal = jnp.tril(jnp.ones((CHUNK, CHUNK), bool))
    strict = jnp.tril(jnp.ones((CHUNK, CHUNK), bool), -1)
    decay = jnp.where(causal, jnp.exp(jnp.where(causal, gc[..., :, None] - gc[..., None, :], 0.0)), 0.0)
    kb = kc * bc[..., None]
    a = jnp.where(strict, jnp.einsum('nbhid,nbhjd->nbhij', kb, kc) * decay, 0.0)
    eye = jnp.eye(CHUNK, dtype=F32)
    t_inv = lax.linalg.triangular_solve(eye + a, jnp.broadcast_to(eye, a.shape), left_side=True, lower=True)
    u = jnp.einsum('nbhij,nbhjd->nbhid', t_inv, vc * bc[..., None])
    w = jnp.einsum('nbhij,nbhjd->nbhid', t_inv, kb * jnp.exp(gc)[..., None])
    a_qk = jnp.where(causal, jnp.einsum('nbhid,nbhjd->nbhij', qc, kc) * decay, 0.0)

    def step(s, xs):
        qi, ki, ui, wi, gi, ai = xs
        v_new = ui - jnp.einsum('bhcd,bhde->bhce', wi, s)
        o = (jnp.einsum('bhcd,bhde->bhce', qi * jnp.exp(gi)[..., None], s)
             + jnp.einsum('bhij,bhje->bhie', ai, v_new))
        g_last = gi[..., -1:]
        s = (s * jnp.exp(g_last)[..., None]
             + jnp.einsum('bhcd,bhce->bhde', ki * jnp.exp(g_last - gi)[..., None], v_new))
        return s, o

    s_fin, o = lax.scan(step, s0.astype(F32), (qc, kc, u, w, gc, a_qk))
    return o.transpose(1, 0, 3, 2, 4).reshape(B, L, H, dv), s_fin


def _deltanet_mixer(qkv, z, b_fb, a_fb, conv_w, a_log, dt_bias, norm_g, s0_f, s0_b):
    B, L, _ = qkv.shape
    qkv = jax.nn.silu(_depthwise_conv(qkv, conv_w))
    q, k, v = jnp.split(qkv, 3, axis=-1)
    q = _l2_normalize(q.reshape(B, L, DN_HEADS, DN_DK))
    k = _l2_normalize(k.reshape(B, L, DN_HEADS, DN_DK))
    v = v.reshape(B, L, DN_HEADS, DN_DV)
    beta = jax.nn.sigmoid(b_fb.astype(F32)).reshape(B, L, 2, DN_HEADS)
    g = -jnp.exp(a_log.astype(F32)) * jax.nn.softplus(a_fb.astype(F32).reshape(B, L, 2, DN_HEADS) + dt_bias)
    o_f, s_f = _gated_delta_chunked(q, k, v, g[:, :, 0], beta[:, :, 0], s0_f)
    flip = lambda t: jnp.flip(t, axis=1)
    o_b, s_b = _gated_delta_chunked(flip(q), flip(k), flip(v), flip(g[:, :, 1]), flip(beta[:, :, 1]), s0_b)
    o = o_f + flip(o_b)
    o = _rms_norm(o, norm_g) * jax.nn.silu(z.astype(F32).reshape(B, L, DN_HEADS, DN_DV))
    return o.reshape(B, L, GROUP_W).astype(qkv.dtype), s_f, s_b


def kernel(x_prompt, x_sample, cache_da_k, cache_da_v, cache_na_k, cache_na_v, state_dn_fwd, state_dn_bwd,
           c, c_ctx, w_in, w_out, ada_w, ada_b, norm1_g, norm2_g, da_lam_q1, da_lam_k1, da_lam_q2, da_lam_k2,
           da_norm_g, cv_dw_w, cv_dw_b, cv_gn_g, cv_gn_b, cv_pw_w, cv_pw_b, na_rpb, dn_conv_w, dn_a_log,
           dn_dt_bias, dn_norm_g, moe_wg, moe_bg, moe_we, moe_be, moe_w1, moe_w3, moe_w2, final_norm_g):
    x_all = jnp.concatenate([x_prompt.reshape(M_PROMPT, D_MODEL), x_sample.reshape(M_SAMPLE, D_MODEL)], axis=0)
    cvec16 = jnp.concatenate([c_ctx[None, :], c, jnp.zeros((16 - N_SEG, D_MODEL), F32)], axis=0)
    mod_all = _ada_all(cvec16, ada_w, ada_b)[:, :N_SEG].reshape(DEPTH, N_SEG, 6, D_MODEL)
    rope_tabs = _rope_tables()
    cache_da_k2 = cache_da_k.reshape(DEC_BATCH, DEPTH, PAST_LEN, GROUP_W)
    cache_da_v2 = cache_da_v.reshape(DEC_BATCH, DEPTH, PAST_LEN, GROUP_W)

    new_da_k, new_da_v, new_na_k, new_na_v, new_sf, new_sb = [], [], [], [], [], []
    y_final = None
    for l in range(DEPTH):
        mod = mod_all[l]
        w_main = w_in[l, :, :N_MAIN].astype(BF16)
        w_small = jnp.pad(w_in[l, :, N_MAIN:], ((0, 0), (0, LANES - N_SMALL))).astype(BF16)
        proj, proj_small = _input_projection(x_all, norm1_g[l], mod, w_main, w_small)

        pp = proj[:M_PROMPT]
        new_da_k.append(pp[:, 512:1024].reshape(BATCH, SEQ, DA_HEADS, 2 * DA_DQ))
        new_da_v.append(pp[:, 1024:1536].reshape(BATCH, SEQ, DA_HEADS, DA_DV))
        new_na_k.append(pp[:, 3072:3584].reshape(BATCH, SEQ, NA_HEADS, NA_DH))
        new_na_v.append(pp[:, 3584:4096].reshape(BATCH, SEQ, NA_HEADS, NA_DH))

        lam_init = 0.8 - 0.6 * math.exp(-0.3 * l)
        lam_vec = jnp.stack([da_lam_q1[l], da_lam_k1[l], da_lam_q2[l], da_lam_k2[l]], axis=0)
        o_a = _diff_attention_prompt(proj, lam_vec, da_norm_g[l], lam_init)
        o_a = _diff_attention_sample(proj, cache_da_k2, cache_da_v2, l, rope_tabs, lam_vec, da_norm_g[l],
                                     lam_init, o_a)

        ps = proj[M_PROMPT:]
        cv = lambda u: _conv_module(u, cv_dw_w[l], cv_dw_b[l], cv_gn_g[l], cv_gn_b[l], cv_pw_w[l], cv_pw_b[l])
        o_b = jnp.concatenate([cv(pp[:, 1536:2560].reshape(BATCH, SEQ, 2 * GROUP_W)).reshape(M_PROMPT, GROUP_W),
                               cv(ps[:, 1536:2560].reshape(DEC_BATCH, DEC_SEQ, 2 * GROUP_W)).reshape(M_SAMPLE, GROUP_W)],
                              axis=0)

        o_c_p = _softmax_attention_prompt(proj)[:M_PROMPT]
        nq = ps[:, 2560:3072].reshape(DEC_BATCH, DEC_SEQ, NA_HEADS, NA_DH)
        nk = ps[:, 3072:3584].reshape(DEC_BATCH, DEC_SEQ, NA_HEADS, NA_DH)
        nv = ps[:, 3584:4096].reshape(DEC_BATCH, DEC_SEQ, NA_HEADS, NA_DH)
        o_c_s = _neighborhood_attention(nq, nk, nv, cache_na_k[:, l], cache_na_v[:, l], na_rpb[l])
        o_c = jnp.concatenate([o_c_p, o_c_s.reshape(M_SAMPLE, GROUP_W)], axis=0)

        small_p = proj_small[:M_PROMPT]
        small_s = proj_small[M_PROMPT:]
        zeros_state = jnp.zeros((BATCH, DN_HEADS, DN_DK, DN_DV), F32)
        o_d_p, s_f, s_b = _deltanet_mixer(
            pp[:, 4096:5632].reshape(BATCH, SEQ, 3 * GROUP_W), pp[:, 5632:6144].reshape(BATCH, SEQ, GROUP_W),
            small_p[:, 0:8].reshape(BATCH, SEQ, 8), small_p[:, 8:16].reshape(BATCH, SEQ, 8),
            dn_conv_w[l], dn_a_log[l], dn_dt_bias[l], dn_norm_g[l], zeros_state, zeros_state)
        new_sf.append(s_f)
        new_sb.append(s_b)
        o_d_s, _, _ = _deltanet_mixer(
            ps[:, 4096:5632].reshape(DEC_BATCH, DEC_SEQ, 3 * GROUP_W),
            ps[:, 5632:6144].reshape(DEC_BATCH, DEC_SEQ, GROUP_W),
            small_s[:, 0:8].reshape(DEC_BATCH, DEC_SEQ, 8), small_s[:, 8:16].reshape(DEC_BATCH, DEC_SEQ, 8),
            dn_conv_w[l], dn_a_log[l], dn_dt_bias[l], dn_norm_g[l], state_dn_fwd[:, l], state_dn_bwd[:, l])
        o_d = jnp.concatenate([o_d_p.reshape(M_PROMPT, GROUP_W), o_d_s.reshape(M_SAMPLE, GROUP_W)], axis=0)

        w_r = jnp.pad(jnp.concatenate([moe_wg[l], moe_we[l]], axis=1),
                      ((0, 0), (0, LANES - N_GROUPS - N_EXPERTS)))
        wr_hi, wr_lo = _split_bf16(w_r)
        b_r = jnp.pad(jnp.concatenate([moe_bg[l], moe_be[l]]), (0, LANES - N_GROUPS - N_EXPERTS)).reshape(1, LANES)
        x_all, h2, logits = _output_projection(o_a, o_b, o_c, o_d, x_all, w_out[l].astype(BF16), mod,
                                               norm2_g[l], wr_hi, wr_lo, b_r)

        expert_id, gates = _route(logits)
        slot_tok, blk_expert, n_used, dest = _dispatch_plan(expert_id)
        xb = h2[slot_tok]
        yb = _moe_experts(blk_expert, n_used, xb, moe_w1, moe_w3, moe_w2, l)
        y_pair = yb[dest].reshape(M_ALL, TOP_K * D_MODEL)
        last = l == DEPTH - 1
        res = _combine(x_all, y_pair, gates, mod, final_norm_g, last)
        x_all = res[0]
        if last:
            y_final = res[1]

    y_prompt = y_final[:M_PROMPT].reshape(BATCH, SEQ, D_MODEL)
    y_sample = y_final[M_PROMPT:].reshape(DEC_BATCH, DEC_SEQ, D_MODEL)
    return (y_prompt, y_sample,
            jnp.stack(new_da_k, axis=1), jnp.stack(new_da_v, axis=1),
            jnp.stack(new_na_k, axis=1), jnp.stack(new_na_v, axis=1),
            jnp.stack(new_sf, axis=1), jnp.stack(new_sb, axis=1))
```

```python
import functools
import math

import numpy as np
import jax
import jax.numpy as jnp
from jax import lax
from jax.experimental import pallas as pl
from jax.experimental.pallas import tpu as pltpu

F32 = jnp.float32
BF16 = jnp.bfloat16

D_MODEL = 2048
BATCH = 16
SEQ = 256
DEPTH = 4
DEC_BATCH = 8
DEC_SEQ = 1024
PAST_LEN = 512
GRID_W = 64
GROUP_W = 512
DA_HEADS = 4
DA_DV = 128
DA_DQ = 64
ROPE_BASE = 10000.0
CONV_W = 31
CONV_GROUPS = 4
NA_HEADS = 4
NA_DH = 128
WIN_R = 8
WIN_C = 16
NA_QCOLS = 16
NA_KCOLS = 2 * WIN_C
DN_HEADS = 4
DN_DK = 128
DN_DV = 128
DN_CONV_W = 3
CHUNK = 64
N_GROUPS = 4
E_PER_GROUP = 8
N_EXPERTS = 32
TOP_K = 2
D_FF_EXPERT = 512
EPS = 1e-6
NEG_INF = -1e30
N_MAIN = 12 * GROUP_W
N_SMALL = 4 * DN_HEADS

M_PROMPT = BATCH * SEQ
M_SAMPLE = DEC_BATCH * DEC_SEQ
M_ALL = M_PROMPT + M_SAMPLE
N_SEG = 1 + DEC_BATCH

LANES = 128
VMEM_LIMIT = 56 * 1024 * 1024

COL_DA_Q, COL_DA_K, COL_DA_V = 0, 4, 8
COL_CV_U = 12
COL_NA_Q, COL_NA_K, COL_NA_V = 20, 24, 28
COL_DN_QKV, COL_DN_Z = 32, 44


def _cparams(sem):
    return pltpu.CompilerParams(dimension_semantics=sem, vmem_limit_bytes=VMEM_LIMIT)


def _seg_of_tile(i, tm):
    n_prompt_tiles = M_PROMPT // tm
    per_batch = DEC_SEQ // tm
    return jnp.where(i < n_prompt_tiles, 0, 1 + (i - n_prompt_tiles) // per_batch)


ADA_TN = 1536
ADA_KS = 256


def _ada_kernel(c_ref, w_ref, b_ref, o_ref):
    c = c_ref[...]
    s = (c * jax.nn.sigmoid(c)).astype(BF16)
    acc = jnp.zeros((16, ADA_TN), F32)
    for k in range(D_MODEL // ADA_KS):
        w = w_ref[k * ADA_KS:(k + 1) * ADA_KS, :].astype(BF16)
        acc = acc + jnp.dot(s[:, k * ADA_KS:(k + 1) * ADA_KS], w, preferred_element_type=F32)
    o_ref[...] = acc + b_ref[...]


def _ada_all(cvec16, ada_w, ada_b):
    n = 6 * D_MODEL
    return pl.pallas_call(
        _ada_kernel,
        out_shape=jax.ShapeDtypeStruct((DEPTH, 16, n), F32),
        grid=(DEPTH, n // ADA_TN),
        in_specs=[pl.BlockSpec((16, D_MODEL), lambda l, j: (0, 0)),
                  pl.BlockSpec((None, D_MODEL, ADA_TN), lambda l, j: (l, 0, j)),
                  pl.BlockSpec((None, 1, ADA_TN), lambda l, j: (l, 0, j))],
        out_specs=pl.BlockSpec((None, 16, ADA_TN), lambda l, j: (l, 0, j)),
        compiler_params=_cparams(("arbitrary", "arbitrary")),
        name="ada_mod",
    )(cvec16, ada_w, ada_b.reshape(DEPTH, 1, n))


K1_TM = 1024
K1_TN = 512
ROW_CHUNK = 128


def _modulated_norm(x, g, sc, sh):
    ms = jnp.mean(x * x, axis=-1, keepdims=True)
    return (x * lax.rsqrt(ms + EPS)) * g * (1.0 + sc) + sh


def _k1_kernel(x_ref, g_ref, mod_ref, w_ref, ws_ref, o_ref, os_ref, h_scr):
    j = pl.program_id(1)

    @pl.when(j == 0)
    def _():
        g = g_ref[...]
        sh = mod_ref[0:1, :]
        sc = mod_ref[1:2, :]

        def body(r, carry):
            rows = pl.ds(pl.multiple_of(r * ROW_CHUNK, ROW_CHUNK), ROW_CHUNK)
            h = _modulated_norm(x_ref[rows, :], g, sc, sh).astype(BF16)
            h_scr[rows, :] = h
            os_ref[rows, :] = jnp.dot(h, ws_ref[...], preferred_element_type=F32)
            return carry

        lax.fori_loop(0, K1_TM // ROW_CHUNK, body, 0)

    o_ref[...] = jnp.dot(h_scr[...], w_ref[...], preferred_element_type=F32)


def _input_projection(x_all, norm_g, mod, w_main, w_small):
    return pl.pallas_call(
        _k1_kernel,
        out_shape=(jax.ShapeDtypeStruct((M_ALL, N_MAIN), F32),
                   jax.ShapeDtypeStruct((M_ALL, LANES), F32)),
        grid=(M_ALL // K1_TM, N_MAIN // K1_TN),
        in_specs=[pl.BlockSpec((K1_TM, D_MODEL), lambda i, j: (i, 0)),
                  pl.BlockSpec((1, D_MODEL), lambda i, j: (0, 0)),
                  pl.BlockSpec((None, 6, D_MODEL), lambda i, j: (_seg_of_tile(i, K1_TM), 0, 0)),
                  pl.BlockSpec((D_MODEL, K1_TN), lambda i, j: (0, j)),
                  pl.BlockSpec((D_MODEL, LANES), lambda i, j: (0, 0))],
        out_specs=(pl.BlockSpec((K1_TM, K1_TN), lambda i, j: (i, j)),
                   pl.BlockSpec((K1_TM, LANES), lambda i, j: (i, 0))),
        scratch_shapes=[pltpu.VMEM((K1_TM, D_MODEL), BF16)],
        compiler_params=_cparams(("arbitrary", "arbitrary")),
        name="input_proj",
    )(x_all, norm_g.reshape(1, D_MODEL), mod, w_main, w_small)


DA_QB = 256


def _rope_tables():
    t = jnp.arange(DEC_SEQ)
    nf = DA_DQ // 4
    freqs = ROPE_BASE ** (-jnp.arange(nf, dtype=F32) / nf)
    lane = np.arange(LANES)
    m = lane % DA_DQ
    use_col = (m // (DA_DQ // 2)) == 1
    within = m % (DA_DQ // 2)
    fi = within % nf
    second = within >= nf
    pos = jnp.where(jnp.asarray(use_col)[None, :], (t % GRID_W)[:, None], (t // GRID_W)[:, None]).astype(F32)
    ang = pos * freqs[jnp.asarray(fi)][None, :]
    cos = jnp.cos(ang)
    sin = jnp.sin(ang)
    sec = jnp.asarray(second)[None, :]
    sin_next = jnp.where(sec, 0.0, -sin)
    sin_prev = jnp.where(sec, sin, 0.0)
    return cos.astype(F32), sin_next.astype(F32), sin_prev.astype(F32)


def _rope(x, cos, sin_next, sin_prev):
    nf = DA_DQ // 4
    return (x * cos + pltpu.roll(x, LANES - nf, 1) * sin_next + pltpu.roll(x, nf, 1) * sin_prev)


def _softmax_rows(s):
    m = jnp.max(s, axis=-1, keepdims=True)
    e = jnp.exp(s - m)
    return e / jnp.sum(e, axis=-1, keepdims=True)


_NT = (((1,), (1,)), ((), ()))


def _da_kernel(*refs, seq, ctx_len, lam_init):
    if ctx_len:
        (q_ref, k_ref, v_ref, kc_ref, vc_ref, cos_ref, sn_ref, sp_ref, lam_ref, g_ref, _, o_ref,
         k_scr, v_scr) = refs
    else:
        q_ref, k_ref, v_ref, lam_ref, g_ref, o_ref, k_scr, v_scr = refs
    lam_vec = lam_ref[...]
    lam = (jnp.exp(jnp.sum(lam_vec[0:1, :] * lam_vec[1:2, :], axis=-1, keepdims=True))
           - jnp.exp(jnp.sum(lam_vec[2:3, :] * lam_vec[3:4, :], axis=-1, keepdims=True)) + lam_init)
    if ctx_len:
        k_scr[0:seq, :] = _rope(k_ref[...], cos_ref[...], sn_ref[...], sp_ref[...]).astype(BF16)
        k_scr[seq:seq + ctx_len, :] = kc_ref[...].astype(BF16)
        v_scr[0:seq, :] = v_ref[...].astype(BF16)
        v_scr[seq:seq + ctx_len, :] = vc_ref[...].astype(BF16)
    else:
        k_scr[...] = k_ref[...].astype(BF16)
        v_scr[...] = v_ref[...].astype(BF16)
    lane = lax.broadcasted_iota(jnp.int32, (DA_QB, LANES), 1)
    first_map = lane < DA_DQ
    scale = DA_DQ ** -0.5
    gain = g_ref[...] * (1.0 - lam_init)

    def body(qi, carry):
        rows = pl.ds(pl.multiple_of(qi * DA_QB, DA_QB), DA_QB)
        q = q_ref[rows, :]
        if ctx_len:
            q = _rope(q, cos_ref[rows, :], sn_ref[rows, :], sp_ref[rows, :])
        q = q * scale
        q1 = jnp.where(first_map, q, 0.0).astype(BF16)
        q2 = jnp.where(first_map, 0.0, q).astype(BF16)
        kk = k_scr[...]
        p1 = _softmax_rows(lax.dot_general(q1, kk, _NT, preferred_element_type=F32))
        p2 = _softmax_rows(lax.dot_general(q2, kk, _NT, preferred_element_type=F32))
        p = (p1 - lam * p2).astype(BF16)
        o = jnp.dot(p, v_scr[...], preferred_element_type=F32)
        ms = jnp.mean(o * o, axis=-1, keepdims=True)
        o_ref[rows, :] = (o * lax.rsqrt(ms + EPS)) * gain
        return carry

    lax.fori_loop(0, seq // DA_QB, body, 0)


def _diff_attention_prompt(proj, lam_vec, norm_g, lam_init):
    kern = functools.partial(_da_kernel, seq=SEQ, ctx_len=0, lam_init=lam_init)
    blk = lambda c0: pl.BlockSpec((SEQ, LANES), lambda b, h: (b, c0 + h))
    return pl.pallas_call(
        kern,
        out_shape=jax.ShapeDtypeStruct((M_ALL, GROUP_W), F32),
        grid=(BATCH, DA_HEADS),
        in_specs=[blk(COL_DA_Q), blk(COL_DA_K), blk(COL_DA_V),
                  pl.BlockSpec((4, DA_DQ), lambda b, h: (0, 0)),
                  pl.BlockSpec((1, DA_DV), lambda b, h: (0, 0))],
        out_specs=pl.BlockSpec((SEQ, LANES), lambda b, h: (b, h)),
        scratch_shapes=[pltpu.VMEM((SEQ, LANES), BF16), pltpu.VMEM((SEQ, LANES), BF16)],
        compiler_params=_cparams(("arbitrary", "arbitrary")),
        name="diff_attn_prompt",
    )(proj, proj, proj, lam_vec, norm_g.reshape(1, DA_DV))


def _diff_attention_sample(proj, cache_k, cache_v, layer, rope_tabs, lam_vec, norm_g, lam_init, out_buf):
    kern = functools.partial(_da_kernel, seq=DEC_SEQ, ctx_len=PAST_LEN, lam_init=lam_init)
    off = M_PROMPT // DEC_SEQ
    blk = lambda c0: pl.BlockSpec((DEC_SEQ, LANES), lambda b, h: (off + b, c0 + h))
    cblk = pl.BlockSpec((None, None, PAST_LEN, LANES), lambda b, h: (b, layer, 0, h))
    tab = pl.BlockSpec((DEC_SEQ, LANES), lambda b, h: (0, 0))
    cos, sn, sp = rope_tabs
    return pl.pallas_call(
        kern,
        out_shape=jax.ShapeDtypeStruct((M_ALL, GROUP_W), F32),
        grid=(DEC_BATCH, DA_HEADS),
        in_specs=[blk(COL_DA_Q), blk(COL_DA_K), blk(COL_DA_V), cblk, cblk, tab, tab, tab,
                  pl.BlockSpec((4, DA_DQ), lambda b, h: (0, 0)),
                  pl.BlockSpec((1, DA_DV), lambda b, h: (0, 0)),
                  pl.BlockSpec(memory_space=pl.ANY)],
        out_specs=pl.BlockSpec((DEC_SEQ, LANES), lambda b, h: (off + b, h)),
        scratch_shapes=[pltpu.VMEM((DEC_SEQ + PAST_LEN, LANES), BF16),
                        pltpu.VMEM((DEC_SEQ + PAST_LEN, LANES), BF16)],
        input_output_aliases={10: 0},
        compiler_params=_cparams(("arbitrary", "arbitrary")),
        name="diff_attn_sample",
    )(proj, proj, proj, cache_k, cache_v, cos, sn, sp, lam_vec, norm_g.reshape(1, DA_DV), out_buf)


def _sm_attn_kernel(q_ref, k_ref, v_ref, o_ref):
    q = (q_ref[...] * (NA_DH ** -0.5)).astype(BF16)
    p = _softmax_rows(lax.dot_general(q, k_ref[...].astype(BF16), _NT, preferred_element_type=F32))
    o_ref[...] = jnp.dot(p.astype(BF16), v_ref[...].astype(BF16), preferred_element_type=F32)


def _softmax_attention_prompt(proj):
    blk = lambda c0: pl.BlockSpec((SEQ, LANES), lambda b, h: (b, c0 + h))
    return pl.pallas_call(
        _sm_attn_kernel,
        out_shape=jax.ShapeDtypeStruct((M_ALL, GROUP_W), F32),
        grid=(BATCH, NA_HEADS),
        in_specs=[blk(COL_NA_Q), blk(COL_NA_K), blk(COL_NA_V)],
        out_specs=pl.BlockSpec((SEQ, LANES), lambda b, h: (b, h)),
        compiler_params=_cparams(("arbitrary", "arbitrary")),
        name="softmax_attn_prompt",
    )(proj, proj, proj)


GRID_ROWS = DEC_SEQ // GRID_W
NA_WIN_KEYS = WIN_R * GRID_W


def _na_bias_table(rpb):
    r = np.arange(GRID_ROWS)
    rs = np.clip(r - WIN_R // 2, 0, GRID_ROWS - WIN_R)
    row_idx = rs[:, None] + np.arange(WIN_R)[None, :] - r[:, None] + WIN_R - 1
    qc = np.arange(GRID_W)[:, None]
    kc = np.arange(GRID_W)[None, :]
    win_start = np.clip(qc - WIN_C // 2, 0, GRID_W - WIN_C)
    allowed = (kc >= win_start) & (kc < win_start + WIN_C)
    col_idx = np.clip(kc - qc + WIN_C - 1, 0, 2 * WIN_C - 2)
    t = rpb[:, jnp.asarray(row_idx)][..., jnp.asarray(col_idx)]
    t = jnp.where(jnp.asarray(allowed)[None, None, None], t, NEG_INF)
    return t.transpose(0, 1, 3, 2, 4).reshape(NA_HEADS, GRID_ROWS, GRID_W, NA_WIN_KEYS).astype(F32)


def _na_kernel(q_ref, k_ref, v_ref, kc_ref, vc_ref, bias_ref, _, o_ref):
    kc = kc_ref[...].astype(BF16)
    vc = vc_ref[...].astype(BF16)
    scale = NA_DH ** -0.5
    for r in range(GRID_ROWS):
        rs = min(max(r - WIN_R // 2, 0), GRID_ROWS - WIN_R)
        q = (q_ref[r * GRID_W:(r + 1) * GRID_W, :] * scale).astype(BF16)
        kw = k_ref[rs * GRID_W:(rs + WIN_R) * GRID_W, :].astype(BF16)
        vw = v_ref[rs * GRID_W:(rs + WIN_R) * GRID_W, :].astype(BF16)
        s_win = lax.dot_general(q, kw, _NT, preferred_element_type=F32) + bias_ref[r]
        s_ctx = lax.dot_general(q, kc, _NT, preferred_element_type=F32)
        m = jnp.maximum(jnp.max(s_win, axis=-1, keepdims=True), jnp.max(s_ctx, axis=-1, keepdims=True))
        e_win = jnp.exp(s_win - m)
        e_ctx = jnp.exp(s_ctx - m)
        inv = 1.0 / (jnp.sum(e_win, axis=-1, keepdims=True) + jnp.sum(e_ctx, axis=-1, keepdims=True))
        o = (jnp.dot((e_win * inv).astype(BF16), vw, preferred_element_type=F32)
             + jnp.dot((e_ctx * inv).astype(BF16), vc, preferred_element_type=F32))
        o_ref[r * GRID_W:(r + 1) * GRID_W, :] = o


def _neighborhood_attention_sample(proj, cache_k, cache_v, layer, bias_tab, out_buf):
    off = M_PROMPT // DEC_SEQ
    blk = lambda c0: pl.BlockSpec((DEC_SEQ, LANES), lambda b, h: (off + b, c0 + h))
    cblk = pl.BlockSpec((None, None, PAST_LEN, LANES), lambda b, h: (b, layer, 0, h))
    return pl.pallas_call(
        _na_kernel,
        out_shape=jax.ShapeDtypeStruct((M_ALL, GROUP_W), F32),
        grid=(DEC_BATCH, NA_HEADS),
        in_specs=[blk(COL_NA_Q), blk(COL_NA_K), blk(COL_NA_V), cblk, cblk,
                  pl.BlockSpec((None, GRID_ROWS, GRID_W, NA_WIN_KEYS), lambda b, h: (h, 0, 0, 0)),
                  pl.BlockSpec(memory_space=pl.ANY)],
        out_specs=pl.BlockSpec((DEC_SEQ, LANES), lambda b, h: (off + b, h)),
        input_output_aliases={6: 0},
        compiler_params=_cparams(("arbitrary", "arbitrary")),
        name="nbr_attn_sample",
    )(proj, proj, proj, cache_k, cache_v, bias_tab, out_buf)


TRI_G = 16
_BATCHED_NN = (((2,), (1,)), ((0,), (0,)))


def _tri_inv_kernel(a_ref, o_ref):
    b = -a_ref[...]
    q = b
    for _ in range(5):
        bb = b.astype(BF16)
        b = lax.dot_general(bb, bb, _BATCHED_NN, preferred_element_type=F32)
        q = q + b + lax.dot_general(q.astype(BF16), b.astype(BF16), _BATCHED_NN, preferred_element_type=F32)
    rows = lax.broadcasted_iota(jnp.int32, (TRI_G, CHUNK, CHUNK), 1)
    cols = lax.broadcasted_iota(jnp.int32, (TRI_G, CHUNK, CHUNK), 2)
    o_ref[...] = q + jnp.where(rows == cols, 1.0, 0.0)


def _unit_lower_inverse(a):
    shape = a.shape
    flat = a.reshape(-1, CHUNK, CHUNK)
    n = flat.shape[0]
    out = pl.pallas_call(
        _tri_inv_kernel,
        out_shape=jax.ShapeDtypeStruct((n, CHUNK, CHUNK), F32),
        grid=(n // TRI_G,),
        in_specs=[pl.BlockSpec((TRI_G, CHUNK, CHUNK), lambda i: (i, 0, 0))],
        out_specs=pl.BlockSpec((TRI_G, CHUNK, CHUNK), lambda i: (i, 0, 0)),
        compiler_params=_cparams(("arbitrary",)),
        name="unit_lower_inverse",
    )(flat)
    return out.reshape(shape)


K2_TM = 512


def _split_bf16(a):
    hi = a.astype(BF16)
    lo = (a - hi.astype(F32)).astype(BF16)
    return hi, lo


def _k2_kernel(oa_ref, ob_ref, oc_ref, od_ref, x_ref, w_ref, mod_ref, g_ref, wr_hi_ref, wr_lo_ref, br_ref,
               xo_ref, h2_ref, lg_ref):
    acc = jnp.dot(oa_ref[...].astype(BF16), w_ref[0 * GROUP_W:1 * GROUP_W, :], preferred_element_type=F32)
    acc += jnp.dot(ob_ref[...].astype(BF16), w_ref[1 * GROUP_W:2 * GROUP_W, :], preferred_element_type=F32)
    acc += jnp.dot(oc_ref[...].astype(BF16), w_ref[2 * GROUP_W:3 * GROUP_W, :], preferred_element_type=F32)
    acc += jnp.dot(od_ref[...].astype(BF16), w_ref[3 * GROUP_W:4 * GROUP_W, :], preferred_element_type=F32)
    x = x_ref[...] + mod_ref[2:3, :] * acc
    xo_ref[...] = x
    h2 = _modulated_norm(x, g_ref[...], mod_ref[4:5, :], mod_ref[3:4, :])
    hi, lo = _split_bf16(h2)
    h2_ref[...] = hi
    lg = (jnp.dot(hi, wr_hi_ref[...], preferred_element_type=F32)
          + jnp.dot(hi, wr_lo_ref[...], preferred_element_type=F32)
          + jnp.dot(lo, wr_hi_ref[...], preferred_element_type=F32))
    lg_ref[...] = lg + br_ref[...]


def _output_projection(oa, ob, oc, od, x_all, w_out_bf16, mod, norm2_g, wr_hi, wr_lo, br):
    mix = pl.BlockSpec((K2_TM, GROUP_W), lambda i: (i, 0))
    row = pl.BlockSpec((K2_TM, D_MODEL), lambda i: (i, 0))
    const = lambda shape: pl.BlockSpec(shape, lambda i: (0,) * len(shape))
    return pl.pallas_call(
        _k2_kernel,
        out_shape=(jax.ShapeDtypeStruct((M_ALL, D_MODEL), F32),
                   jax.ShapeDtypeStruct((M_ALL, D_MODEL), BF16),
                   jax.ShapeDtypeStruct((M_ALL, LANES), F32)),
        grid=(M_ALL // K2_TM,),
        in_specs=[mix, mix, mix, mix, row, const((D_MODEL, D_MODEL)),
                  pl.BlockSpec((None, 6, D_MODEL), lambda i: (_seg_of_tile(i, K2_TM), 0, 0)),
                  const((1, D_MODEL)), const((D_MODEL, LANES)), const((D_MODEL, LANES)), const((1, LANES))],
        out_specs=(row, row, pl.BlockSpec((K2_TM, LANES), lambda i: (i, 0))),
        compiler_params=_cparams(("arbitrary",)),
        name="output_proj",
    )(oa, ob, oc, od, x_all, w_out_bf16, mod, norm2_g.reshape(1, D_MODEL), wr_hi, wr_lo, br)


MOE_TB = 256
N_ASSIGN = M_ALL * TOP_K
MOE_NBLK = N_ASSIGN // MOE_TB + N_EXPERTS


def _moe_kernel(be_ref, nused_ref, x_ref, w1_ref, w3_ref, w2_ref, y_ref, w1_scr, w3_scr, w2_scr):
    i = pl.program_id(0)
    prev = be_ref[jnp.maximum(i - 1, 0)]
    new_expert = jnp.logical_or(i == 0, be_ref[i] != prev)

    @pl.when(new_expert)
    def _():
        w1_scr[...] = w1_ref[...].astype(BF16)
        w3_scr[...] = w3_ref[...].astype(BF16)
        w2_scr[...] = w2_ref[...].astype(BF16)

    @pl.when(i < nused_ref[0])
    def _():
        x = x_ref[...]
        a = jnp.dot(x, w1_scr[...], preferred_element_type=F32)
        b = jnp.dot(x, w3_scr[...], preferred_element_type=F32)
        hmid = (a * jax.nn.sigmoid(a) * b).astype(BF16)
        y_ref[...] = jnp.dot(hmid, w2_scr[...], preferred_element_type=F32)

    @pl.when(i >= nused_ref[0])
    def _():
        y_ref[...] = jnp.zeros_like(y_ref)


def _moe_experts(blk_expert, n_used, xb, w1, w3, w2, layer):
    grid_spec = pltpu.PrefetchScalarGridSpec(
        num_scalar_prefetch=2,
        grid=(MOE_NBLK,),
        in_specs=[pl.BlockSpec((MOE_TB, D_MODEL), lambda i, be, nu: (i, 0)),
                  pl.BlockSpec((None, None, D_MODEL, D_FF_EXPERT), lambda i, be, nu: (layer, be[i], 0, 0)),
                  pl.BlockSpec((None, None, D_MODEL, D_FF_EXPERT), lambda i, be, nu: (layer, be[i], 0, 0)),
                  pl.BlockSpec((None, None, D_FF_EXPERT, D_MODEL), lambda i, be, nu: (layer, be[i], 0, 0))],
        out_specs=pl.BlockSpec((MOE_TB, D_MODEL), lambda i, be, nu: (i, 0)),
        scratch_shapes=[pltpu.VMEM((D_MODEL, D_FF_EXPERT), BF16),
                        pltpu.VMEM((D_MODEL, D_FF_EXPERT), BF16),
                        pltpu.VMEM((D_FF_EXPERT, D_MODEL), BF16)])
    return pl.pallas_call(
        _moe_kernel,
        out_shape=jax.ShapeDtypeStruct((MOE_NBLK * MOE_TB, D_MODEL), F32),
        grid_spec=grid_spec,
        compiler_params=_cparams(("arbitrary",)),
        name="moe_experts",
    )(blk_expert, n_used, xb, w1, w3, w2)


def _route(logits):
    g_logits = logits[:, :N_GROUPS]
    grp = jnp.argmax(g_logits, axis=-1)
    p_grp = jnp.take_along_axis(jax.nn.softmax(g_logits, axis=-1), grp[:, None], axis=-1)
    e_logits = logits[:, N_GROUPS:N_GROUPS + N_EXPERTS].reshape(M_ALL, N_GROUPS, E_PER_GROUP)
    e_logits = jnp.take_along_axis(e_logits, grp[:, None, None], axis=1)[:, 0]
    top_v, top_i = lax.top_k(e_logits, TOP_K)
    gates = jax.nn.softmax(top_v, axis=-1) * p_grp
    expert_id = (grp[:, None] * E_PER_GROUP + top_i).astype(jnp.int32)
    return expert_id, gates


def _dispatch_plan(expert_id):
    flat = expert_id.reshape(-1)
    order = jnp.argsort(flat)
    sorted_e = flat[order]
    counts = jnp.zeros((N_EXPERTS,), jnp.int32).at[flat].add(1)
    padded = (counts + MOE_TB - 1) // MOE_TB * MOE_TB
    pad_end = jnp.cumsum(padded)
    pad_start = pad_end - padded
    start = jnp.cumsum(counts) - counts
    dest_sorted = pad_start[sorted_e] + jnp.arange(N_ASSIGN, dtype=jnp.int32) - start[sorted_e]
    slot_tok = jnp.zeros((MOE_NBLK * MOE_TB,), jnp.int32).at[dest_sorted].set((order // TOP_K).astype(jnp.int32))
    n_used = (pad_end[-1] // MOE_TB).astype(jnp.int32)
    blk_start = jnp.minimum(jnp.arange(MOE_NBLK, dtype=jnp.int32), n_used - 1) * MOE_TB
    blk_expert = jnp.minimum(jnp.searchsorted(pad_end, blk_start, side='right'), N_EXPERTS - 1).astype(jnp.int32)
    dest = jnp.zeros((N_ASSIGN,), jnp.int32).at[order].set(dest_sorted)
    return slot_tok, blk_expert, n_used.reshape(1), dest


CB_TM = 512


def _combine_kernel(x_ref, y_ref, gate_ref, mod_ref, fg_ref, xo_ref, *maybe_norm_ref):
    gates = gate_ref[...]
    y = y_ref[:, 0:D_MODEL] * gates[:, 0:1] + y_ref[:, D_MODEL:2 * D_MODEL] * gates[:, 1:2]
    x = x_ref[...] + mod_ref[5:6, :] * y
    xo_ref[...] = x
    if maybe_norm_ref:
        ms = jnp.mean(x * x, axis=-1, keepdims=True)
        maybe_norm_ref[0][...] = (x * lax.rsqrt(ms + EPS)) * fg_ref[...]


def _combine(x_all, y_pair, gates, mod, final_g, with_final_norm):
    row = pl.BlockSpec((CB_TM, D_MODEL), lambda i: (i, 0))
    out_shape = [jax.ShapeDtypeStruct((M_ALL, D_MODEL), F32)]
    out_specs = [row]
    if with_final_norm:
        out_shape.append(jax.ShapeDtypeStruct((M_ALL, D_MODEL), F32))
        out_specs.append(row)
    return pl.pallas_call(
        _combine_kernel,
        out_shape=tuple(out_shape),
        grid=(M_ALL // CB_TM,),
        in_specs=[row, pl.BlockSpec((CB_TM, 2 * D_MODEL), lambda i: (i, 0)),
                  pl.BlockSpec((CB_TM, TOP_K), lambda i: (i, 0)),
                  pl.BlockSpec((None, 6, D_MODEL), lambda i: (_seg_of_tile(i, CB_TM), 0, 0)),
                  pl.BlockSpec((1, D_MODEL), lambda i: (0, 0))],
        out_specs=tuple(out_specs),
        compiler_params=_cparams(("arbitrary",)),
        name="moe_combine_final" if with_final_norm else "moe_combine",
    )(x_all, y_pair, gates, mod, final_g.reshape(1, D_MODEL))


def _rms_norm(x, g):
    xf = x.astype(F32)
    y = xf * lax.rsqrt(jnp.mean(xf * xf, axis=-1, keepdims=True) + EPS)
    return y.astype(x.dtype) * g


def _l2_normalize(x):
    xf = x.astype(F32)
    return xf * lax.rsqrt(jnp.sum(xf * xf, axis=-1, keepdims=True) + EPS)


def _depthwise_conv(x, w):
    k = w.shape[0]
    return lax.conv_general_dilated(x, w[:, None, :], (1,), ((k // 2, k // 2),),
                                    dimension_numbers=('NWC', 'WIO', 'NWC'),
                                    feature_group_count=x.shape[-1])


def _conv_module(u, dw_w, dw_b, gn_g, gn_b, pw_w, pw_b):
    a, gate = jnp.split(u, 2, axis=-1)
    h = _depthwise_conv(a * jax.nn.sigmoid(gate), dw_w) + dw_b
    B, L, C = h.shape
    hf = h.astype(F32).reshape(B, L, CONV_GROUPS, C // CONV_GROUPS)
    mu = jnp.mean(hf, axis=-1, keepdims=True)
    var = jnp.mean(jnp.square(hf - mu), axis=-1, keepdims=True)
    hn = ((hf - mu) * lax.rsqrt(var + EPS)).reshape(B, L, C).astype(h.dtype) * gn_g + gn_b
    return jax.nn.silu(hn) @ pw_w + pw_b


def _gated_delta_chunked(q, k, v, g, beta, s0, reverse=False):
    B, L, H, dk = q.shape
    dv = v.shape[-1]
    n = L // CHUNK
    qc = (q.astype(F32) * dk ** -0.5).reshape(B, n, CHUNK, H, dk).transpose(1, 0, 3, 2, 4)
    kc = k.astype(F32).reshape(B, n, CHUNK, H, dk).transpose(1, 0, 3, 2, 4)
    vc = v.astype(F32).reshape(B, n, CHUNK, H, dv).transpose(1, 0, 3, 2, 4)
    gc = lax.cumsum(g.astype(F32).reshape(B, n, CHUNK, H).transpose(1, 0, 3, 2), axis=3, reverse=reverse)
    bc = beta.astype(F32).reshape(B, n, CHUNK, H).transpose(1, 0, 3, 2)
    ones = jnp.ones((CHUNK, CHUNK), bool)
    causal = jnp.triu(ones) if reverse else jnp.tril(ones)
    strict = jnp.triu(ones, 1) if reverse else jnp.tril(ones, -1)
    decay = jnp.where(causal, jnp.exp(jnp.where(causal, gc[..., :, None] - gc[..., None, :], 0.0)), 0.0)
    kb = kc * bc[..., None]
    a = jnp.where(strict, jnp.einsum('nbhid,nbhjd->nbhij', kb, kc) * decay, 0.0)
    t_inv = _unit_lower_inverse(a)
    u =jnp.einsum('nbhij,nbhjd->nbhid', t_inv, vc * bc[..., None])
    w = jnp.einsum('nbhij,nbhjd->nbhid', t_inv, kb * jnp.exp(gc)[..., None])
    a_qk = jnp.where(causal, jnp.einsum('nbhid,nbhjd->nbhij', qc, kc) * decay, 0.0)

    def step(s, xs):
        qi, ki, ui, wi, gi, ai = xs
        v_new = ui - jnp.einsum('bhcd,bhde->bhce', wi, s)
        o = (jnp.einsum('bhcd,bhde->bhce', qi * jnp.exp(gi)[..., None], s)
             + jnp.einsum('bhij,bhje->bhie', ai, v_new))
        g_last = gi[..., :1] if reverse else gi[..., -1:]
        s = (s * jnp.exp(g_last)[..., None]
             + jnp.einsum('bhcd,bhce->bhde', ki * jnp.exp(g_last - gi)[..., None], v_new))
        return s, o

    s_fin, o = lax.scan(step, s0.astype(F32), (qc, kc, u, w, gc, a_qk), reverse=reverse)
    return o.transpose(1, 0, 3, 2, 4).reshape(B, L, H, dv), s_fin


def _deltanet_mixer(qkv, z, b_fb, a_fb, conv_w, a_log, dt_bias, norm_g, s0_f, s0_b):
    B, L, _ = qkv.shape
    qkv = jax.nn.silu(_depthwise_conv(qkv, conv_w))
    q, k, v = jnp.split(qkv, 3, axis=-1)
    q = _l2_normalize(q.reshape(B, L, DN_HEADS, DN_DK))
    k = _l2_normalize(k.reshape(B, L, DN_HEADS, DN_DK))
    v = v.reshape(B, L, DN_HEADS, DN_DV)
    beta = jax.nn.sigmoid(b_fb.astype(F32)).reshape(B, L, 2, DN_HEADS)
    g = -jnp.exp(a_log.astype(F32)) * jax.nn.softplus(a_fb.astype(F32).reshape(B, L, 2, DN_HEADS) + dt_bias)
    o_f, s_f = _gated_delta_chunked(q, k, v, g[:, :, 0], beta[:, :, 0], s0_f)
    o_b, s_b = _gated_delta_chunked(q, k, v, g[:, :, 1], beta[:, :, 1], s0_b, reverse=True)
    o = o_f + o_b
    o = _rms_norm(o, norm_g) * jax.nn.silu(z.astype(F32).reshape(B, L, DN_HEADS, DN_DV))
    return o.reshape(B, L, GROUP_W).astype(qkv.dtype), s_f, s_b


def kernel(x_prompt, x_sample, cache_da_k, cache_da_v, cache_na_k, cache_na_v, state_dn_fwd, state_dn_bwd,
           c, c_ctx, w_in, w_out, ada_w, ada_b, norm1_g, norm2_g, da_lam_q1, da_lam_k1, da_lam_q2, da_lam_k2,
           da_norm_g, cv_dw_w, cv_dw_b, cv_gn_g, cv_gn_b, cv_pw_w, cv_pw_b, na_rpb, dn_conv_w, dn_a_log,
           dn_dt_bias, dn_norm_g, moe_wg, moe_bg, moe_we, moe_be, moe_w1, moe_w3, moe_w2, final_norm_g):
    x_all = jnp.concatenate([x_prompt.reshape(M_PROMPT, D_MODEL), x_sample.reshape(M_SAMPLE, D_MODEL)], axis=0)
    cvec16 = jnp.concatenate([c_ctx[None, :], c, jnp.zeros((16 - N_SEG, D_MODEL), F32)], axis=0)
    mod_all = _ada_all(cvec16, ada_w, ada_b)[:, :N_SEG].reshape(DEPTH, N_SEG, 6, D_MODEL)
    rope_tabs = _rope_tables()
    cache_da_k2 = cache_da_k.reshape(DEC_BATCH, DEPTH, PAST_LEN, GROUP_W)
    cache_da_v2 = cache_da_v.reshape(DEC_BATCH, DEPTH, PAST_LEN, GROUP_W)
    cache_na_k2 = cache_na_k.reshape(DEC_BATCH, DEPTH, PAST_LEN, GROUP_W)
    cache_na_v2 = cache_na_v.reshape(DEC_BATCH, DEPTH, PAST_LEN, GROUP_W)

    new_da_k, new_da_v, new_na_k, new_na_v, new_sf, new_sb = [], [], [], [], [], []
    y_final = None
    for l in range(DEPTH):
        mod = mod_all[l]
        w_main = w_in[l, :, :N_MAIN].astype(BF16)
        w_small = jnp.pad(w_in[l, :, N_MAIN:], ((0, 0), (0, LANES - N_SMALL))).astype(BF16)
        proj, proj_small = _input_projection(x_all, norm1_g[l], mod, w_main, w_small)

        pp = proj[:M_PROMPT]
        new_da_k.append(pp[:, 512:1024].reshape(BATCH, SEQ, DA_HEADS, 2 * DA_DQ))
        new_da_v.append(pp[:, 1024:1536].reshape(BATCH, SEQ, DA_HEADS, DA_DV))
        new_na_k.append(pp[:, 3072:3584].reshape(BATCH, SEQ, NA_HEADS, NA_DH))
        new_na_v.append(pp[:, 3584:4096].reshape(BATCH, SEQ, NA_HEADS, NA_DH))

        lam_init = 0.8 - 0.6 * math.exp(-0.3 * l)
        lam_vec = jnp.stack([da_lam_q1[l], da_lam_k1[l], da_lam_q2[l], da_lam_k2[l]], axis=0)
        o_a = _diff_attention_prompt(proj, lam_vec, da_norm_g[l], lam_init)
        o_a = _diff_attention_sample(proj, cache_da_k2, cache_da_v2, l, rope_tabs, lam_vec, da_norm_g[l],
                                     lam_init, o_a)

        ps = proj[M_PROMPT:]
        cv = lambda u: _conv_module(u, cv_dw_w[l], cv_dw_b[l], cv_gn_g[l], cv_gn_b[l], cv_pw_w[l], cv_pw_b[l])
        o_b = jnp.concatenate([cv(pp[:, 1536:2560].reshape(BATCH, SEQ, 2 * GROUP_W)).reshape(M_PROMPT, GROUP_W),
                               cv(ps[:, 1536:2560].reshape(DEC_BATCH, DEC_SEQ, 2 * GROUP_W)).reshape(M_SAMPLE, GROUP_W)],
                              axis=0)

        o_c = _softmax_attention_prompt(proj)
        o_c = _neighborhood_attention_sample(proj, cache_na_k2, cache_na_v2, l, _na_bias_table(na_rpb[l]), o_c)

        small_p = proj_small[:M_PROMPT]
        small_s = proj_small[M_PROMPT:]
        zeros_state = jnp.zeros((BATCH, DN_HEADS, DN_DK, DN_DV), F32)
        o_d_p, s_f, s_b = _deltanet_mixer(
            pp[:, 4096:5632].reshape(BATCH, SEQ, 3 * GROUP_W), pp[:, 5632:6144].reshape(BATCH, SEQ, GROUP_W),
            small_p[:, 0:8].reshape(BATCH, SEQ, 8), small_p[:, 8:16].reshape(BATCH, SEQ, 8),
            dn_conv_w[l], dn_a_log[l], dn_dt_bias[l], dn_norm_g[l], zeros_state, zeros_state)
        new_sf.append(s_f)
        new_sb.append(s_b)
        o_d_s, _, _ = _deltanet_mixer(
            ps[:, 4096:5632].reshape(DEC_BATCH, DEC_SEQ, 3 * GROUP_W),
            ps[:, 5632:6144].reshape(DEC_BATCH, DEC_SEQ, GROUP_W),
            small_s[:, 0:8].reshape(DEC_BATCH, DEC_SEQ, 8), small_s[:, 8:16].reshape(DEC_BATCH, DEC_SEQ, 8),
            dn_conv_w[l], dn_a_log[l], dn_dt_bias[l], dn_norm_g[l], state_dn_fwd[:, l], state_dn_bwd[:, l])
        o_d = jnp.concatenate([o_d_p.reshape(M_PROMPT, GROUP_W), o_d_s.reshape(M_SAMPLE, GROUP_W)], axis=0)

        w_r = jnp.pad(jnp.concatenate([moe_wg[l], moe_we[l]], axis=1),
                      ((0, 0), (0, LANES - N_GROUPS - N_EXPERTS)))
        wr_hi, wr_lo = _split_bf16(w_r)
        b_r = jnp.pad(jnp.concatenate([moe_bg[l], moe_be[l]]), (0, LANES - N_GROUPS - N_EXPERTS)).reshape(1, LANES)
        x_all, h2, logits = _output_projection(o_a, o_b, o_c, o_d, x_all, w_out[l].astype(BF16), mod,
                                               norm2_g[l], wr_hi, wr_lo, b_r)

        expert_id, gates = _route(logits)
        slot_tok, blk_expert, n_used, dest = _dispatch_plan(expert_id)
        xb = h2[slot_tok]
        yb = _moe_experts(blk_expert, n_used, xb, moe_w1, moe_w3, moe_w2, l)
        y_pair = yb[dest].reshape(M_ALL, TOP_K * D_MODEL)
        last = l == DEPTH - 1
        res = _combine(x_all, y_pair, gates, mod, final_norm_g, last)
        x_all = res[0]
        if last:
            y_final = res[1]

    y_prompt = y_final[:M_PROMPT].reshape(BATCH, SEQ, D_MODEL)
    y_sample = y_final[M_PROMPT:].reshape(DEC_BATCH, DEC_SEQ, D_MODEL)
    return (y_prompt, y_sample,
            jnp.stack(new_da_k, axis=1), jnp.stack(new_da_v, axis=1),
            jnp.stack(new_na_k, axis=1), jnp.stack(new_na_v, axis=1),
            jnp.stack(new_sf, axis=1), jnp.stack(new_sb, axis=1))
```

```python
import functools
import math

import numpy as np
import jax
import jax.numpy as jnp
from jax import lax
from jax.experimental import pallas as pl
from jax.experimental.pallas import tpu as pltpu

F32 = jnp.float32
BF16 = jnp.bfloat16

D_MODEL = 2048
BATCH = 16
SEQ = 256
DEPTH = 4
DEC_BATCH = 8
DEC_SEQ = 1024
PAST_LEN = 512
GRID_W = 64
GROUP_W = 512
DA_HEADS = 4
DA_DV = 128
DA_DQ = 64
ROPE_BASE = 10000.0
CONV_W = 31
CONV_GROUPS = 4
NA_HEADS = 4
NA_DH = 128
WIN_R = 8
WIN_C = 16
NA_QCOLS = 16
NA_KCOLS = 2 * WIN_C
DN_HEADS = 4
DN_DK = 128
DN_DV = 128
DN_CONV_W = 3
CHUNK = 64
N_GROUPS = 4
E_PER_GROUP = 8
N_EXPERTS = 32
TOP_K = 2
D_FF_EXPERT = 512
EPS = 1e-6
NEG_INF = -1e30
N_MAIN = 12 * GROUP_W
N_SMALL = 4 * DN_HEADS

M_PROMPT = BATCH * SEQ
M_SAMPLE = DEC_BATCH * DEC_SEQ
M_ALL = M_PROMPT + M_SAMPLE
N_SEG = 1 + DEC_BATCH

LANES = 128
VMEM_LIMIT = 56 * 1024 * 1024

COL_DA_Q, COL_DA_K, COL_DA_V = 0, 4, 8
COL_CV_U = 12
COL_NA_Q, COL_NA_K, COL_NA_V = 20, 24, 28
COL_DN_QKV, COL_DN_Z = 32, 44


def _cparams(sem):
    return pltpu.CompilerParams(dimension_semantics=sem, vmem_limit_bytes=VMEM_LIMIT)


def _seg_of_tile(i, tm):
    n_prompt_tiles = M_PROMPT // tm
    per_batch = DEC_SEQ // tm
    return jnp.where(i < n_prompt_tiles, 0, 1 + (i - n_prompt_tiles) // per_batch)


ADA_TN = 1536
ADA_KS = 256


def _ada_kernel(c_ref, w_ref, b_ref, o_ref):
    c = c_ref[...]
    s = (c * jax.nn.sigmoid(c)).astype(BF16)
    acc = jnp.zeros((16, ADA_TN), F32)
    for k in range(D_MODEL // ADA_KS):
        w = w_ref[k * ADA_KS:(k + 1) * ADA_KS, :].astype(BF16)
        acc = acc + jnp.dot(s[:, k * ADA_KS:(k + 1) * ADA_KS], w, preferred_element_type=F32)
    o_ref[...] = acc + b_ref[...]


def _ada_all(cvec16, ada_w, ada_b):
    n = 6 * D_MODEL
    return pl.pallas_call(
        _ada_kernel,
        out_shape=jax.ShapeDtypeStruct((DEPTH, 16, n), F32),
        grid=(DEPTH, n // ADA_TN),
        in_specs=[pl.BlockSpec((16, D_MODEL), lambda l, j: (0, 0)),
                  pl.BlockSpec((None, D_MODEL, ADA_TN), lambda l, j: (l, 0, j)),
                  pl.BlockSpec((None, 1, ADA_TN), lambda l, j: (l, 0, j))],
        out_specs=pl.BlockSpec((None, 16, ADA_TN), lambda l, j: (l, 0, j)),
        compiler_params=_cparams(("arbitrary", "arbitrary")),
        name="ada_mod",
    )(cvec16, ada_w, ada_b.reshape(DEPTH, 1, n))


K1_TM = 1024
K1_TN = 512
ROW_CHUNK = 128


def _modulated_norm(x, g, sc, sh):
    ms = jnp.mean(x * x, axis=-1, keepdims=True)
    return (x * lax.rsqrt(ms + EPS)) * g * (1.0 + sc) + sh


def _k1_kernel(x_ref, g_ref, mod_ref, w_ref, ws_ref, o_ref, os_ref, h_scr):
    j = pl.program_id(1)

    @pl.when(j == 0)
    def _():
        g = g_ref[...]
        sh = mod_ref[0:1, :]
        sc = mod_ref[1:2, :]

        def body(r, carry):
            rows = pl.ds(pl.multiple_of(r * ROW_CHUNK, ROW_CHUNK), ROW_CHUNK)
            h = _modulated_norm(x_ref[rows, :], g, sc, sh).astype(BF16)
            h_scr[rows, :] = h
            os_ref[rows, :] = jnp.dot(h, ws_ref[...], preferred_element_type=F32)
            return carry

        lax.fori_loop(0, K1_TM // ROW_CHUNK, body, 0)

    o_ref[...] = jnp.dot(h_scr[...], w_ref[...], preferred_element_type=F32)


def _input_projection(x_all, norm_g, mod, w_main, w_small):
    return pl.pallas_call(
        _k1_kernel,
        out_shape=(jax.ShapeDtypeStruct((M_ALL, N_MAIN), F32),
                   jax.ShapeDtypeStruct((M_ALL, LANES), F32)),
        grid=(M_ALL // K1_TM, N_MAIN // K1_TN),
        in_specs=[pl.BlockSpec((K1_TM, D_MODEL), lambda i, j: (i, 0)),
                  pl.BlockSpec((1, D_MODEL), lambda i, j: (0, 0)),
                  pl.BlockSpec((None, 6, D_MODEL), lambda i, j: (_seg_of_tile(i, K1_TM), 0, 0)),
                  pl.BlockSpec((D_MODEL, K1_TN), lambda i, j: (0, j)),
                  pl.BlockSpec((D_MODEL, LANES), lambda i, j: (0, 0))],
        out_specs=(pl.BlockSpec((K1_TM, K1_TN), lambda i, j: (i, j)),
                   pl.BlockSpec((K1_TM, LANES), lambda i, j: (i, 0))),
        scratch_shapes=[pltpu.VMEM((K1_TM, D_MODEL), BF16)],
        compiler_params=_cparams(("arbitrary", "arbitrary")),
        name="input_proj",
    )(x_all, norm_g.reshape(1, D_MODEL), mod, w_main, w_small)


DA_QB = 256


def _rope_tables():
    t = jnp.arange(DEC_SEQ)
    nf = DA_DQ // 4
    freqs = ROPE_BASE ** (-jnp.arange(nf, dtype=F32) / nf)
    lane = np.arange(LANES)
    m = lane % DA_DQ
    use_col = (m // (DA_DQ // 2)) == 1
    within = m % (DA_DQ // 2)
    fi = within % nf
    second = within >= nf
    pos = jnp.where(jnp.asarray(use_col)[None, :], (t % GRID_W)[:, None], (t // GRID_W)[:, None]).astype(F32)
    ang = pos * freqs[jnp.asarray(fi)][None, :]
    cos = jnp.cos(ang)
    sin = jnp.sin(ang)
    sec = jnp.asarray(second)[None, :]
    sin_next = jnp.where(sec, 0.0, -sin)
    sin_prev = jnp.where(sec, sin, 0.0)
    return cos.astype(F32), sin_next.astype(F32), sin_prev.astype(F32)


def _rope(x, cos, sin_next, sin_prev):
    nf = DA_DQ // 4
    return (x * cos + pltpu.roll(x, LANES - nf, 1) * sin_next + pltpu.roll(x, nf, 1) * sin_prev)


def _softmax_rows(s):
    m = jnp.max(s, axis=-1, keepdims=True)
    e = jnp.exp(s - m)
    return e / jnp.sum(e, axis=-1, keepdims=True)


_NT = (((1,), (1,)), ((), ()))


def _da_kernel(*refs, seq, ctx_len, lam_init):
    if ctx_len:
        (q_ref, k_ref, v_ref, kc_ref, vc_ref, cos_ref, sn_ref, sp_ref, lam_ref, g_ref, _, o_ref,
         k_scr, v_scr) = refs
    else:
        q_ref, k_ref, v_ref, lam_ref, g_ref, o_ref, k_scr, v_scr = refs
    lam_vec = lam_ref[...]
    lam = (jnp.exp(jnp.sum(lam_vec[0:1, :] * lam_vec[1:2, :], axis=-1, keepdims=True))
           - jnp.exp(jnp.sum(lam_vec[2:3, :] * lam_vec[3:4, :], axis=-1, keepdims=True)) + lam_init)
    if ctx_len:
        k_scr[0:seq, :] = _rope(k_ref[...], cos_ref[...], sn_ref[...], sp_ref[...]).astype(BF16)
        k_scr[seq:seq + ctx_len, :] = kc_ref[...].astype(BF16)
        v_scr[0:seq, :] = v_ref[...].astype(BF16)
        v_scr[seq:seq + ctx_len, :] = vc_ref[...].astype(BF16)
    else:
        k_scr[...] = k_ref[...].astype(BF16)
        v_scr[...] = v_ref[...].astype(BF16)
    lane = lax.broadcasted_iota(jnp.int32, (DA_QB, LANES), 1)
    first_map = lane < DA_DQ
    scale = DA_DQ ** -0.5
    gain = g_ref[...] * (1.0 - lam_init)

    def body(qi, carry):
        rows = pl.ds(pl.multiple_of(qi * DA_QB, DA_QB), DA_QB)
        q = q_ref[rows, :]
        if ctx_len:
            q = _rope(q, cos_ref[rows, :], sn_ref[rows, :], sp_ref[rows, :])
        q = q * scale
        q1 = jnp.where(first_map, q, 0.0).astype(BF16)
        q2 = jnp.where(first_map, 0.0, q).astype(BF16)
        kk = k_scr[...]
        p1 = _softmax_rows(lax.dot_general(q1, kk, _NT, preferred_element_type=F32))
        p2 = _softmax_rows(lax.dot_general(q2, kk, _NT, preferred_element_type=F32))
        p = (p1 - lam * p2).astype(BF16)
        o = jnp.dot(p, v_scr[...], preferred_element_type=F32)
        ms = jnp.mean(o * o, axis=-1, keepdims=True)
        o_ref[rows, :] = (o * lax.rsqrt(ms + EPS)) * gain
        return carry

    lax.fori_loop(0, seq // DA_QB, body, 0)


def _diff_attention_prompt(proj, lam_vec, norm_g, lam_init):
    kern = functools.partial(_da_kernel, seq=SEQ, ctx_len=0, lam_init=lam_init)
    blk = lambda c0: pl.BlockSpec((SEQ, LANES), lambda b, h: (b, c0 + h))
    return pl.pallas_call(
        kern,
        out_shape=jax.ShapeDtypeStruct((M_ALL, GROUP_W), F32),
        grid=(BATCH, DA_HEADS),
        in_specs=[blk(COL_DA_Q), blk(COL_DA_K), blk(COL_DA_V),
                  pl.BlockSpec((4, DA_DQ), lambda b, h: (0, 0)),
                  pl.BlockSpec((1, DA_DV), lambda b, h: (0, 0))],
        out_specs=pl.BlockSpec((SEQ, LANES), lambda b, h: (b, h)),
        scratch_shapes=[pltpu.VMEM((SEQ, LANES), BF16), pltpu.VMEM((SEQ, LANES), BF16)],
        compiler_params=_cparams(("arbitrary", "arbitrary")),
        name="diff_attn_prompt",
    )(proj, proj, proj, lam_vec, norm_g.reshape(1, DA_DV))


def _diff_attention_sample(proj, cache_k, cache_v, layer, rope_tabs, lam_vec, norm_g, lam_init, out_buf):
    kern = functools.partial(_da_kernel, seq=DEC_SEQ, ctx_len=PAST_LEN, lam_init=lam_init)
    off = M_PROMPT // DEC_SEQ
    blk = lambda c0: pl.BlockSpec((DEC_SEQ, LANES), lambda b, h: (off + b, c0 + h))
    cblk = pl.BlockSpec((None, None, PAST_LEN, LANES), lambda b, h: (b, layer, 0, h))
    tab = pl.BlockSpec((DEC_SEQ, LANES), lambda b, h: (0, 0))
    cos, sn, sp = rope_tabs
    return pl.pallas_call(
        kern,
        out_shape=jax.ShapeDtypeStruct((M_ALL, GROUP_W), F32),
        grid=(DEC_BATCH, DA_HEADS),
        in_specs=[blk(COL_DA_Q), blk(COL_DA_K), blk(COL_DA_V), cblk, cblk, tab, tab, tab,
                  pl.BlockSpec((4, DA_DQ), lambda b, h: (0, 0)),
                  pl.BlockSpec((1, DA_DV), lambda b, h: (0, 0)),
                  pl.BlockSpec(memory_space=pl.ANY)],
        out_specs=pl.BlockSpec((DEC_SEQ, LANES), lambda b, h: (off + b, h)),
        scratch_shapes=[pltpu.VMEM((DEC_SEQ + PAST_LEN, LANES), BF16),
                        pltpu.VMEM((DEC_SEQ + PAST_LEN, LANES), BF16)],
        input_output_aliases={10: 0},
        compiler_params=_cparams(("arbitrary", "arbitrary")),
        name="diff_attn_sample",
    )(proj, proj, proj, cache_k, cache_v, cos, sn, sp, lam_vec, norm_g.reshape(1, DA_DV), out_buf)


def _sm_attn_kernel(q_ref, k_ref, v_ref, o_ref):
    q = (q_ref[...] * (NA_DH ** -0.5)).astype(BF16)
    p = _softmax_rows(lax.dot_general(q, k_ref[...].astype(BF16), _NT, preferred_element_type=F32))
    o_ref[...] = jnp.dot(p.astype(BF16), v_ref[...].astype(BF16), preferred_element_type=F32)


def _softmax_attention_prompt(proj):
    blk = lambda c0: pl.BlockSpec((SEQ, LANES), lambda b, h: (b, c0 + h))
    return pl.pallas_call(
        _sm_attn_kernel,
        out_shape=jax.ShapeDtypeStruct((M_ALL, GROUP_W), F32),
        grid=(BATCH, NA_HEADS),
        in_specs=[blk(COL_NA_Q), blk(COL_NA_K), blk(COL_NA_V)],
        out_specs=pl.BlockSpec((SEQ, LANES), lambda b, h: (b, h)),
        compiler_params=_cparams(("arbitrary", "arbitrary")),
        name="softmax_attn_prompt",
    )(proj, proj, proj)


GRID_ROWS = DEC_SEQ // GRID_W
NA_WIN_KEYS = WIN_R * GRID_W


def _na_bias_table(rpb):
    r = np.arange(GRID_ROWS)
    rs = np.clip(r - WIN_R // 2, 0, GRID_ROWS - WIN_R)
    row_idx = rs[:, None] + np.arange(WIN_R)[None, :] - r[:, None] + WIN_R - 1
    qc = np.arange(GRID_W)[:, None]
    kc = np.arange(GRID_W)[None, :]
    win_start = np.clip(qc - WIN_C // 2, 0, GRID_W - WIN_C)
    allowed = (kc >= win_start) & (kc < win_start + WIN_C)
    col_idx = np.clip(kc - qc + WIN_C - 1, 0, 2 * WIN_C - 2)
    t = rpb[:, jnp.asarray(row_idx)][..., jnp.asarray(col_idx)]
    t = jnp.where(jnp.asarray(allowed)[None, None, None], t, NEG_INF)
    return t.transpose(0, 1, 3, 2, 4).reshape(NA_HEADS, GRID_ROWS, GRID_W, NA_WIN_KEYS).astype(F32)


def _na_kernel(q_ref, k_ref, v_ref, kc_ref, vc_ref, bias_ref, _, o_ref):
    kc = kc_ref[...].astype(BF16)
    vc = vc_ref[...].astype(BF16)
    scale = NA_DH ** -0.5
    for r in range(GRID_ROWS):
        rs = min(max(r - WIN_R // 2, 0), GRID_ROWS - WIN_R)
        q = (q_ref[r * GRID_W:(r + 1) * GRID_W, :] * scale).astype(BF16)
        kw = k_ref[rs * GRID_W:(rs + WIN_R) * GRID_W, :].astype(BF16)
        vw = v_ref[rs * GRID_W:(rs + WIN_R) * GRID_W, :].astype(BF16)
        s_win = lax.dot_general(q, kw, _NT, preferred_element_type=F32) + bias_ref[r]
        s_ctx = lax.dot_general(q, kc, _NT, preferred_element_type=F32)
        m = jnp.maximum(jnp.max(s_win, axis=-1, keepdims=True), jnp.max(s_ctx, axis=-1, keepdims=True))
        e_win = jnp.exp(s_win - m)
        e_ctx = jnp.exp(s_ctx - m)
        inv = 1.0 / (jnp.sum(e_win, axis=-1, keepdims=True) + jnp.sum(e_ctx, axis=-1, keepdims=True))
        o = (jnp.dot((e_win * inv).astype(BF16), vw, preferred_element_type=F32)
             + jnp.dot((e_ctx * inv).astype(BF16), vc, preferred_element_type=F32))
        o_ref[r * GRID_W:(r + 1) * GRID_W, :] = o


def _neighborhood_attention_sample(proj, cache_k, cache_v, layer, bias_tab, out_buf):
    off = M_PROMPT // DEC_SEQ
    blk = lambda c0: pl.BlockSpec((DEC_SEQ, LANES), lambda b, h: (off + b, c0 + h))
    cblk = pl.BlockSpec((None, None, PAST_LEN, LANES), lambda b, h: (b, layer, 0, h))
    return pl.pallas_call(
        _na_kernel,
        out_shape=jax.ShapeDtypeStruct((M_ALL, GROUP_W), F32),
        grid=(DEC_BATCH, NA_HEADS),
        in_specs=[blk(COL_NA_Q), blk(COL_NA_K), blk(COL_NA_V), cblk, cblk,
                  pl.BlockSpec((None, GRID_ROWS, GRID_W, NA_WIN_KEYS), lambda b, h: (h, 0, 0, 0)),
                  pl.BlockSpec(memory_space=pl.ANY)],
        out_specs=pl.BlockSpec((DEC_SEQ, LANES), lambda b, h: (off + b, h)),
        input_output_aliases={6: 0},
        compiler_params=_cparams(("arbitrary", "arbitrary")),
        name="nbr_attn_sample",
    )(proj, proj, proj, cache_k, cache_v, bias_tab, out_buf)


TRI_G = 16
_BATCHED_NN = (((2,), (1,)), ((0,), (0,)))


def _tri_inv_kernel(a_ref, o_ref):
    b = -a_ref[...]
    q = b
    for _ in range(5):
        bb = b.astype(BF16)
        b = lax.dot_general(bb, bb, _BATCHED_NN, preferred_element_type=F32)
        q = q + b + lax.dot_general(q.astype(BF16), b.astype(BF16), _BATCHED_NN, preferred_element_type=F32)
    rows = lax.broadcasted_iota(jnp.int32, (TRI_G, CHUNK, CHUNK), 1)
    cols = lax.broadcasted_iota(jnp.int32, (TRI_G, CHUNK, CHUNK), 2)
    o_ref[...] = q + jnp.where(rows == cols, 1.0, 0.0)


def _unit_lower_inverse(a):
    shape = a.shape
    flat = a.reshape(-1, CHUNK, CHUNK)
    n = flat.shape[0]
    out = pl.pallas_call(
        _tri_inv_kernel,
        out_shape=jax.ShapeDtypeStruct((n, CHUNK, CHUNK), F32),
        grid=(n // TRI_G,),
        in_specs=[pl.BlockSpec((TRI_G, CHUNK, CHUNK), lambda i: (i, 0, 0))],
        out_specs=pl.BlockSpec((TRI_G, CHUNK, CHUNK), lambda i: (i, 0, 0)),
        compiler_params=_cparams(("arbitrary",)),
        name="unit_lower_inverse",
    )(flat)
    return out.reshape(shape)


K2_TM = 256


def _split_bf16(a):
    hi = a.astype(BF16)
    lo = (a - hi.astype(F32)).astype(BF16)
    return hi, lo


def _k2_kernel(oa_ref, ob_ref, oc_ref, od_ref, x_ref, w_ref, mod_ref, g_ref, wr_hi_ref, wr_lo_ref, br_ref,
               xo_ref, h2_ref, lg_ref):
    acc = jnp.dot(oa_ref[...].astype(BF16), w_ref[0 * GROUP_W:1 * GROUP_W, :], preferred_element_type=F32)
    acc += jnp.dot(ob_ref[...].astype(BF16), w_ref[1 * GROUP_W:2 * GROUP_W, :], preferred_element_type=F32)
    acc += jnp.dot(oc_ref[...].astype(BF16), w_ref[2 * GROUP_W:3 * GROUP_W, :], preferred_element_type=F32)
    acc += jnp.dot(od_ref[...].astype(BF16), w_ref[3 * GROUP_W:4 * GROUP_W, :], preferred_element_type=F32)
    x = x_ref[...] + mod_ref[2:3, :] * acc
    xo_ref[...] = x
    h2 = _modulated_norm(x, g_ref[...], mod_ref[4:5, :], mod_ref[3:4, :])
    hi, lo = _split_bf16(h2)
    h2_ref[...] = h2
    lg = (jnp.dot(hi, wr_hi_ref[...], preferred_element_type=F32)
          + jnp.dot(hi, wr_lo_ref[...], preferred_element_type=F32)
          + jnp.dot(lo, wr_hi_ref[...], preferred_element_type=F32))
    lg_ref[...] = lg + br_ref[...]


def _output_projection(oa, ob, oc, od, x_all, w_out_bf16, mod, norm2_g, wr_hi, wr_lo, br):
    mix = pl.BlockSpec((K2_TM, GROUP_W), lambda i: (i, 0))
    row = pl.BlockSpec((K2_TM, D_MODEL), lambda i: (i, 0))
    const = lambda shape: pl.BlockSpec(shape, lambda i: (0,) * len(shape))
    return pl.pallas_call(
        _k2_kernel,
        out_shape=(jax.ShapeDtypeStruct((M_ALL, D_MODEL), F32),
                   jax.ShapeDtypeStruct((M_ALL, D_MODEL), F32),
                   jax.ShapeDtypeStruct((M_ALL, LANES), F32)),
        grid=(M_ALL // K2_TM,),
        in_specs=[mix, mix, mix, mix, row, const((D_MODEL, D_MODEL)),
                  pl.BlockSpec((None, 6, D_MODEL), lambda i: (_seg_of_tile(i, K2_TM), 0, 0)),
                  const((1, D_MODEL)), const((D_MODEL, LANES)), const((D_MODEL, LANES)), const((1, LANES))],
        out_specs=(row, row, pl.BlockSpec((K2_TM, LANES), lambda i: (i, 0))),
        compiler_params=_cparams(("arbitrary",)),
        name="output_proj",
    )(oa, ob, oc, od, x_all, w_out_bf16, mod, norm2_g.reshape(1, D_MODEL), wr_hi, wr_lo, br)


MOE_TB = 256
N_ASSIGN = M_ALL * TOP_K
MOE_NBLK = N_ASSIGN // MOE_TB + N_EXPERTS


def _moe_kernel(be_ref, nused_ref, x_ref, w1_ref, w3_ref, w2_ref, y_ref, w1_scr, w3_scr, w2_scr):
    i = pl.program_id(0)
    prev = be_ref[jnp.maximum(i - 1, 0)]
    new_expert = jnp.logical_or(i == 0, be_ref[i] != prev)

    @pl.when(new_expert)
    def _():
        w1_scr[...] = w1_ref[...].astype(BF16)
        w3_scr[...] = w3_ref[...].astype(BF16)
        w2_scr[...] = w2_ref[...].astype(BF16)

    @pl.when(i < nused_ref[0])
    def _():
        x = x_ref[...].astype(BF16)
        a = jnp.dot(x, w1_scr[...], preferred_element_type=F32)
        b = jnp.dot(x, w3_scr[...], preferred_element_type=F32)
        hmid = (a * jax.nn.sigmoid(a) * b).astype(BF16)
        y_ref[...] = jnp.dot(hmid, w2_scr[...], preferred_element_type=F32)

    @pl.when(i >= nused_ref[0])
    def _():
        y_ref[...] = jnp.zeros_like(y_ref)


def _moe_experts(blk_expert, n_used, xb, w1, w3, w2, layer):
    grid_spec = pltpu.PrefetchScalarGridSpec(
        num_scalar_prefetch=2,
        grid=(MOE_NBLK,),
        in_specs=[pl.BlockSpec((MOE_TB, D_MODEL), lambda i, be, nu: (i, 0)),
                  pl.BlockSpec((None, None, D_MODEL, D_FF_EXPERT), lambda i, be, nu: (layer, be[i], 0, 0)),
                  pl.BlockSpec((None, None, D_MODEL, D_FF_EXPERT), lambda i, be, nu: (layer, be[i], 0, 0)),
                  pl.BlockSpec((None, None, D_FF_EXPERT, D_MODEL), lambda i, be, nu: (layer, be[i], 0, 0))],
        out_specs=pl.BlockSpec((MOE_TB, D_MODEL), lambda i, be, nu: (i, 0)),
        scratch_shapes=[pltpu.VMEM((D_MODEL, D_FF_EXPERT), BF16),
                        pltpu.VMEM((D_MODEL, D_FF_EXPERT), BF16),
                        pltpu.VMEM((D_FF_EXPERT, D_MODEL), BF16)])
    return pl.pallas_call(
        _moe_kernel,
        out_shape=jax.ShapeDtypeStruct((MOE_NBLK * MOE_TB, D_MODEL), F32),
        grid_spec=grid_spec,
        compiler_params=_cparams(("arbitrary",)),
        name="moe_experts",
    )(blk_expert, n_used, xb, w1, w3, w2)


def _route(logits):
    g_logits = logits[:, :N_GROUPS]
    grp = jnp.argmax(g_logits, axis=-1)
    p_grp = jnp.take_along_axis(jax.nn.softmax(g_logits, axis=-1), grp[:, None], axis=-1)
    e_logits = logits[:, N_GROUPS:N_GROUPS + N_EXPERTS].reshape(M_ALL, N_GROUPS, E_PER_GROUP)
    e_logits = jnp.take_along_axis(e_logits, grp[:, None, None], axis=1)[:, 0]
    top_v, top_i = lax.top_k(e_logits, TOP_K)
    gates = jax.nn.softmax(top_v, axis=-1) * p_grp
    expert_id = (grp[:, None] * E_PER_GROUP + top_i).astype(jnp.int32)
    return expert_id, gates


def _dispatch_plan(expert_id):
    flat = expert_id.reshape(-1)
    order = jnp.argsort(flat)
    sorted_e = flat[order]
    counts = jnp.zeros((N_EXPERTS,), jnp.int32).at[flat].add(1)
    padded = (counts + MOE_TB - 1) // MOE_TB * MOE_TB
    pad_end = jnp.cumsum(padded)
    pad_start = pad_end - padded
    start = jnp.cumsum(counts) - counts
    dest_sorted = pad_start[sorted_e] + jnp.arange(N_ASSIGN, dtype=jnp.int32) - start[sorted_e]
    slot_tok = jnp.zeros((MOE_NBLK * MOE_TB,), jnp.int32).at[dest_sorted].set((order // TOP_K).astype(jnp.int32))
    n_used = (pad_end[-1] // MOE_TB).astype(jnp.int32)
    blk_start = jnp.minimum(jnp.arange(MOE_NBLK, dtype=jnp.int32), n_used - 1) * MOE_TB
    blk_expert = jnp.minimum(jnp.searchsorted(pad_end, blk_start, side='right'), N_EXPERTS - 1).astype(jnp.int32)
    dest = jnp.zeros((N_ASSIGN,), jnp.int32).at[order].set(dest_sorted)
    return slot_tok, blk_expert, n_used.reshape(1), dest


CB_TM = 512


def _combine_kernel(x_ref, y0_ref, y1_ref, gate_ref, mod_ref, fg_ref, xo_ref, *maybe_norm_ref):
    gates = gate_ref[...]
    y = y0_ref[...] * gates[:, 0:1] + y1_ref[...] * gates[:, 1:2]
    x = x_ref[...] + mod_ref[5:6, :] * y
    xo_ref[...] = x
    if maybe_norm_ref:
        ms = jnp.mean(x * x, axis=-1, keepdims=True)
        maybe_norm_ref[0][...] = (x * lax.rsqrt(ms + EPS)) * fg_ref[...]


def _combine(x_all, y0, y1, gates, mod, final_g, with_final_norm):
    row = pl.BlockSpec((CB_TM, D_MODEL), lambda i: (i, 0))
    out_shape = [jax.ShapeDtypeStruct((M_ALL, D_MODEL), F32)]
    out_specs = [row]
    if with_final_norm:
        out_shape.append(jax.ShapeDtypeStruct((M_ALL, D_MODEL), F32))
        out_specs.append(row)
    return pl.pallas_call(
        _combine_kernel,
        out_shape=tuple(out_shape),
        grid=(M_ALL // CB_TM,),
        in_specs=[row, row, row,
                  pl.BlockSpec((CB_TM, TOP_K), lambda i: (i, 0)),
                  pl.BlockSpec((None, 6, D_MODEL), lambda i: (_seg_of_tile(i, CB_TM), 0, 0)),
                  pl.BlockSpec((1, D_MODEL), lambda i: (0, 0))],
        out_specs=tuple(out_specs),
        compiler_params=_cparams(("arbitrary",)),
        name="moe_combine_final" if with_final_norm else "moe_combine",
    )(x_all, y0, y1, gates, mod, final_g.reshape(1, D_MODEL))


def _rms_norm(x, g):
    xf = x.astype(F32)
    y = xf * lax.rsqrt(jnp.mean(xf * xf, axis=-1, keepdims=True) + EPS)
    return y.astype(x.dtype) * g


def _l2_normalize(x):
    xf = x.astype(F32)
    return xf * lax.rsqrt(jnp.sum(xf * xf, axis=-1, keepdims=True) + EPS)


def _depthwise_conv(x, w):
    k = w.shape[0]
    return lax.conv_general_dilated(x, w[:, None, :], (1,), ((k // 2, k // 2),),
                                    dimension_numbers=('NWC', 'WIO', 'NWC'),
                                    feature_group_count=x.shape[-1])


def _conv_module(u, dw_w, dw_b, gn_g, gn_b, pw_w, pw_b):
    a, gate = jnp.split(u, 2, axis=-1)
    h = _depthwise_conv(a * jax.nn.sigmoid(gate), dw_w) + dw_b
    B, L, C = h.shape
    hf = h.astype(F32).reshape(B, L, CONV_GROUPS, C // CONV_GROUPS)
    mu = jnp.mean(hf, axis=-1, keepdims=True)
    var = jnp.mean(jnp.square(hf - mu), axis=-1, keepdims=True)
    hn = ((hf - mu) * lax.rsqrt(var + EPS)).reshape(B, L, C).astype(h.dtype) * gn_g + gn_b
    return jax.nn.silu(hn) @ pw_w + pw_b


def _gated_delta_chunked(q, k, v, g, beta, s0, reverse=False):
    B, L, H, dk = q.shape
    dv = v.shape[-1]
    n = L // CHUNK
    qc = (q.astype(F32) * dk ** -0.5).reshape(B, n, CHUNK, H, dk).transpose(1, 0, 3, 2, 4)
    kc = k.astype(F32).reshape(B, n, CHUNK, H, dk).transpose(1, 0, 3, 2, 4)
    vc = v.astype(F32).reshape(B, n, CHUNK, H, dv).transpose(1, 0, 3, 2, 4)
    gc = lax.cumsum(g.astype(F32).reshape(B, n, CHUNK, H).transpose(1, 0, 3, 2), axis=3, reverse=reverse)
    bc = beta.astype(F32).reshape(B, n, CHUNK, H).transpose(1, 0, 3, 2)
    ones = jnp.ones((CHUNK, CHUNK), bool)
    causal = jnp.triu(ones) if reverse else jnp.tril(ones)
    strict = jnp.triu(ones, 1) if reverse else jnp.tril(ones, -1)
    decay = jnp.where(causal, jnp.exp(jnp.where(causal, gc[..., :, None] - gc[..., None, :], 0.0)), 0.0)
    kb = kc * bc[..., None]
    a = jnp.where(strict, jnp.einsum('nbhid,nbhjd->nbhij', kb, kc) * decay, 0.0)
    t_inv = _unit_lower_inverse(a)
    u =jnp.einsum('nbhij,nbhjd->nbhid', t_inv, vc * bc[..., None])
    w = jnp.einsum('nbhij,nbhjd->nbhid', t_inv, kb * jnp.exp(gc)[..., None])
    a_qk = jnp.where(causal, jnp.einsum('nbhid,nbhjd->nbhij', qc, kc) * decay, 0.0)

    def step(s, xs):
        qi, ki, ui, wi, gi, ai = xs
        v_new = ui - jnp.einsum('bhcd,bhde->bhce', wi, s)
        o = (jnp.einsum('bhcd,bhde->bhce', qi * jnp.exp(gi)[..., None], s)
             + jnp.einsum('bhij,bhje->bhie', ai, v_new))
        g_last = gi[..., :1] if reverse else gi[..., -1:]
        s = (s * jnp.exp(g_last)[..., None]
             + jnp.einsum('bhcd,bhce->bhde', ki * jnp.exp(g_last - gi)[..., None], v_new))
        return s, o

    s_fin, o = lax.scan(step, s0.astype(F32), (qc, kc, u, w, gc, a_qk), reverse=reverse)
    return o.transpose(1, 0, 3, 2, 4).reshape(B, L, H, dv), s_fin


def _deltanet_mixer(qkv, z, b_fb, a_fb, conv_w, a_log, dt_bias, norm_g, s0_f, s0_b):
    B, L, _ = qkv.shape
    qkv = jax.nn.silu(_depthwise_conv(qkv, conv_w))
    q, k, v = jnp.split(qkv, 3, axis=-1)
    q = _l2_normalize(q.reshape(B, L, DN_HEADS, DN_DK))
    k = _l2_normalize(k.reshape(B, L, DN_HEADS, DN_DK))
    v = v.reshape(B, L, DN_HEADS, DN_DV)
    beta = jax.nn.sigmoid(b_fb.astype(F32)).reshape(B, L, 2, DN_HEADS)
    g = -jnp.exp(a_log.astype(F32)) * jax.nn.softplus(a_fb.astype(F32).reshape(B, L, 2, DN_HEADS) + dt_bias)
    o_f, s_f = _gated_delta_chunked(q, k, v, g[:, :, 0], beta[:, :, 0], s0_f)
    o_b, s_b = _gated_delta_chunked(q, k, v, g[:, :, 1], beta[:, :, 1], s0_b, reverse=True)
    o = o_f + o_b
    o = _rms_norm(o, norm_g) * jax.nn.silu(z.astype(F32).reshape(B, L, DN_HEADS, DN_DV))
    return o.reshape(B, L, GROUP_W).astype(qkv.dtype), s_f, s_b


def kernel(x_prompt, x_sample, cache_da_k, cache_da_v, cache_na_k, cache_na_v, state_dn_fwd, state_dn_bwd,
           c, c_ctx, w_in, w_out, ada_w, ada_b, norm1_g, norm2_g, da_lam_q1, da_lam_k1, da_lam_q2, da_lam_k2,
           da_norm_g, cv_dw_w, cv_dw_b, cv_gn_g, cv_gn_b, cv_pw_w, cv_pw_b, na_rpb, dn_conv_w, dn_a_log,
           dn_dt_bias, dn_norm_g, moe_wg, moe_bg, moe_we, moe_be, moe_w1, moe_w3, moe_w2, final_norm_g):
    x_all = jnp.concatenate([x_prompt.reshape(M_PROMPT, D_MODEL), x_sample.reshape(M_SAMPLE, D_MODEL)], axis=0)
    cvec16 = jnp.concatenate([c_ctx[None, :], c, jnp.zeros((16 - N_SEG, D_MODEL), F32)], axis=0)
    mod_all = _ada_all(cvec16, ada_w, ada_b)[:, :N_SEG].reshape(DEPTH, N_SEG, 6, D_MODEL)
    rope_tabs = _rope_tables()
    cache_da_k2 = cache_da_k.reshape(DEC_BATCH, DEPTH, PAST_LEN, GROUP_W)
    cache_da_v2 = cache_da_v.reshape(DEC_BATCH, DEPTH, PAST_LEN, GROUP_W)
    cache_na_k2 = cache_na_k.reshape(DEC_BATCH, DEPTH, PAST_LEN, GROUP_W)
    cache_na_v2 = cache_na_v.reshape(DEC_BATCH, DEPTH, PAST_LEN, GROUP_W)

    new_da_k, new_da_v, new_na_k, new_na_v, new_sf, new_sb = [], [], [], [], [], []
    y_final = None
    for l in range(DEPTH):
        mod = mod_all[l]
        w_main = w_in[l, :, :N_MAIN].astype(BF16)
        w_small = jnp.pad(w_in[l, :, N_MAIN:], ((0, 0), (0, LANES - N_SMALL))).astype(BF16)
        proj, proj_small = _input_projection(x_all, norm1_g[l], mod, w_main, w_small)

        pp = proj[:M_PROMPT]
        new_da_k.append(pp[:, 512:1024].reshape(BATCH, SEQ, DA_HEADS, 2 * DA_DQ))
        new_da_v.append(pp[:, 1024:1536].reshape(BATCH, SEQ, DA_HEADS, DA_DV))
        new_na_k.append(pp[:, 3072:3584].reshape(BATCH, SEQ, NA_HEADS, NA_DH))
        new_na_v.append(pp[:, 3584:4096].reshape(BATCH, SEQ, NA_HEADS, NA_DH))

        lam_init = 0.8 - 0.6 * math.exp(-0.3 * l)
        lam_vec = jnp.stack([da_lam_q1[l], da_lam_k1[l], da_lam_q2[l], da_lam_k2[l]], axis=0)
        o_a = _diff_attention_prompt(proj, lam_vec, da_norm_g[l], lam_init)
        o_a = _diff_attention_sample(proj, cache_da_k2, cache_da_v2, l, rope_tabs, lam_vec, da_norm_g[l],
                                     lam_init, o_a)

        ps = proj[M_PROMPT:]
        cv = lambda u: _conv_module(u, cv_dw_w[l], cv_dw_b[l], cv_gn_g[l], cv_gn_b[l], cv_pw_w[l], cv_pw_b[l])
        o_b = jnp.concatenate([cv(pp[:, 1536:2560].reshape(BATCH, SEQ, 2 * GROUP_W)).reshape(M_PROMPT, GROUP_W),
                               cv(ps[:, 1536:2560].reshape(DEC_BATCH, DEC_SEQ, 2 * GROUP_W)).reshape(M_SAMPLE, GROUP_W)],
                              axis=0)

        o_c = _softmax_attention_prompt(proj)
        o_c = _neighborhood_attention_sample(proj, cache_na_k2, cache_na_v2, l, _na_bias_table(na_rpb[l]), o_c)

        small_p = proj_small[:M_PROMPT]
        small_s = proj_small[M_PROMPT:]
        zeros_state = jnp.zeros((BATCH, DN_HEADS, DN_DK, DN_DV), F32)
        o_d_p, s_f, s_b = _deltanet_mixer(
            pp[:, 4096:5632].reshape(BATCH, SEQ, 3 * GROUP_W), pp[:, 5632:6144].reshape(BATCH, SEQ, GROUP_W),
            small_p[:, 0:8].reshape(BATCH, SEQ, 8), small_p[:, 8:16].reshape(BATCH, SEQ, 8),
            dn_conv_w[l], dn_a_log[l], dn_dt_bias[l], dn_norm_g[l], zeros_state, zeros_state)
        new_sf.append(s_f)
        new_sb.append(s_b)
        o_d_s, _, _ = _deltanet_mixer(
            ps[:, 4096:5632].reshape(DEC_BATCH, DEC_SEQ, 3 * GROUP_W),
            ps[:, 5632:6144].reshape(DEC_BATCH, DEC_SEQ, GROUP_W),
            small_s[:, 0:8].reshape(DEC_BATCH, DEC_SEQ, 8), small_s[:, 8:16].reshape(DEC_BATCH, DEC_SEQ, 8),
            dn_conv_w[l], dn_a_log[l], dn_dt_bias[l], dn_norm_g[l], state_dn_fwd[:, l], state_dn_bwd[:, l])
        o_d = jnp.concatenate([o_d_p.reshape(M_PROMPT, GROUP_W), o_d_s.reshape(M_SAMPLE, GROUP_W)], axis=0)

        w_r = jnp.pad(jnp.concatenate([moe_wg[l], moe_we[l]], axis=1),
                      ((0, 0), (0, LANES - N_GROUPS - N_EXPERTS)))
        wr_hi, wr_lo = _split_bf16(w_r)
        b_r = jnp.pad(jnp.concatenate([moe_bg[l], moe_be[l]]), (0, LANES - N_GROUPS - N_EXPERTS)).reshape(1, LANES)
        x_all, h2, logits = _output_projection(o_a, o_b, o_c, o_d, x_all, w_out[l].astype(BF16), mod,
                                               norm2_g[l], wr_hi, wr_lo, b_r)

        expert_id, gates = _route(logits)
        slot_tok, blk_expert, n_used, dest = _dispatch_plan(expert_id)
        xb = h2[slot_tok]
        yb = _moe_experts(blk_expert, n_used, xb, moe_w1, moe_w3, moe_w2, l)
        dest2 = dest.reshape(M_ALL, TOP_K)
        last = l == DEPTH - 1
        res = _combine(x_all, yb[dest2[:, 0]], yb[dest2[:, 1]], gates, mod, final_norm_g, last)
        x_all = res[0]
        if last:
            y_final = res[1]

    y_prompt = y_final[:M_PROMPT].reshape(BATCH, SEQ, D_MODEL)
    y_sample = y_final[M_PROMPT:].reshape(DEC_BATCH, DEC_SEQ, D_MODEL)
    return (y_prompt, y_sample,
            jnp.stack(new_da_k, axis=1), jnp.stack(new_da_v, axis=1),
            jnp.stack(new_na_k, axis=1), jnp.stack(new_na_v, axis=1),
            jnp.stack(new_sf, axis=1), jnp.stack(new_sb, axis=1))
```

```python
import functools
import math

import numpy as np
import jax
import jax.numpy as jnp
from jax import lax
from jax.experimental import pallas as pl
from jax.experimental.pallas import tpu as pltpu

F32 = jnp.float32
BF16 = jnp.bfloat16

D_MODEL = 2048
BATCH = 16
SEQ = 256
DEPTH = 4
DEC_BATCH = 8
DEC_SEQ = 1024
PAST_LEN = 512
GRID_W = 64
GROUP_W = 512
DA_HEADS = 4
DA_DV = 128
DA_DQ = 64
ROPE_BASE = 10000.0
CONV_W = 31
CONV_GROUPS = 4
NA_HEADS = 4
NA_DH = 128
WIN_R = 8
WIN_C = 16
NA_QCOLS = 16
NA_KCOLS = 2 * WIN_C
DN_HEADS = 4
DN_DK = 128
DN_DV = 128
DN_CONV_W = 3
CHUNK = 64
N_GROUPS = 4
E_PER_GROUP = 8
N_EXPERTS = 32
TOP_K = 2
D_FF_EXPERT = 512
EPS = 1e-6
NEG_INF = -1e30
N_MAIN = 12 * GROUP_W
N_SMALL = 4 * DN_HEADS

M_PROMPT = BATCH * SEQ
M_SAMPLE = DEC_BATCH * DEC_SEQ
M_ALL = M_PROMPT + M_SAMPLE
N_SEG = 1 + DEC_BATCH

LANES = 128
VMEM_LIMIT = 56 * 1024 * 1024

COL_DA_Q, COL_DA_K, COL_DA_V = 0, 4, 8
COL_CV_U = 12
COL_NA_Q, COL_NA_K, COL_NA_V = 20, 24, 28
COL_DN_QKV, COL_DN_Z = 32, 44


def _cparams(sem):
    return pltpu.CompilerParams(dimension_semantics=sem, vmem_limit_bytes=VMEM_LIMIT)


def _seg_of_tile(i, tm):
    n_prompt_tiles = M_PROMPT // tm
    per_batch = DEC_SEQ // tm
    return jnp.where(i < n_prompt_tiles, 0, 1 + (i - n_prompt_tiles) // per_batch)


ADA_TN = 1536
ADA_KS = 256


def _ada_kernel(c_ref, w_ref, b_ref, o_ref):
    c = c_ref[...]
    s = (c * jax.nn.sigmoid(c)).astype(BF16)
    acc = jnp.zeros((16, ADA_TN), F32)
    for k in range(D_MODEL // ADA_KS):
        w = w_ref[k * ADA_KS:(k + 1) * ADA_KS, :].astype(BF16)
        acc = acc + jnp.dot(s[:, k * ADA_KS:(k + 1) * ADA_KS], w, preferred_element_type=F32)
    o_ref[...] = acc + b_ref[...]


def _ada_all(cvec16, ada_w, ada_b):
    n = 6 * D_MODEL
    return pl.pallas_call(
        _ada_kernel,
        out_shape=jax.ShapeDtypeStruct((DEPTH, 16, n), F32),
        grid=(DEPTH, n // ADA_TN),
        in_specs=[pl.BlockSpec((16, D_MODEL), lambda l, j: (0, 0)),
                  pl.BlockSpec((None, D_MODEL, ADA_TN), lambda l, j: (l, 0, j)),
                  pl.BlockSpec((None, 1, ADA_TN), lambda l, j: (l, 0, j))],
        out_specs=pl.BlockSpec((None, 16, ADA_TN), lambda l, j: (l, 0, j)),
        compiler_params=_cparams(("arbitrary", "arbitrary")),
        name="ada_mod",
    )(cvec16, ada_w, ada_b.reshape(DEPTH, 1, n))


K1_TM = 1024
K1_TN = 512
ROW_CHUNK = 128


def _modulated_norm(x, g, sc, sh):
    ms = jnp.mean(x * x, axis=-1, keepdims=True)
    return (x * lax.rsqrt(ms + EPS)) * g * (1.0 + sc) + sh


def _k1_kernel(x_ref, g_ref, mod_ref, w_ref, ws_ref, o_ref, os_ref, h_scr):
    j = pl.program_id(1)

    @pl.when(j == 0)
    def _():
        g = g_ref[...]
        sh = mod_ref[0:1, :]
        sc = mod_ref[1:2, :]

        def body(r, carry):
            rows = pl.ds(pl.multiple_of(r * ROW_CHUNK, ROW_CHUNK), ROW_CHUNK)
            h = _modulated_norm(x_ref[rows, :], g, sc, sh).astype(BF16)
            h_scr[rows, :] = h
            os_ref[rows, :] = jnp.dot(h, ws_ref[...], preferred_element_type=F32)
            return carry

        lax.fori_loop(0, K1_TM // ROW_CHUNK, body, 0)

    o_ref[...] = jnp.dot(h_scr[...], w_ref[...], preferred_element_type=F32)


def _input_projection(x_all, norm_g, mod, w_main, w_small):
    return pl.pallas_call(
        _k1_kernel,
        out_shape=(jax.ShapeDtypeStruct((M_ALL, N_MAIN), F32),
                   jax.ShapeDtypeStruct((M_ALL, LANES), F32)),
        grid=(M_ALL // K1_TM, N_MAIN // K1_TN),
        in_specs=[pl.BlockSpec((K1_TM, D_MODEL), lambda i, j: (i, 0)),
                  pl.BlockSpec((1, D_MODEL), lambda i, j: (0, 0)),
                  pl.BlockSpec((None, 6, D_MODEL), lambda i, j: (_seg_of_tile(i, K1_TM), 0, 0)),
                  pl.BlockSpec((D_MODEL, K1_TN), lambda i, j: (0, j)),
                  pl.BlockSpec((D_MODEL, LANES), lambda i, j: (0, 0))],
        out_specs=(pl.BlockSpec((K1_TM, K1_TN), lambda i, j: (i, j)),
                   pl.BlockSpec((K1_TM, LANES), lambda i, j: (i, 0))),
        scratch_shapes=[pltpu.VMEM((K1_TM, D_MODEL), BF16)],
        compiler_params=_cparams(("arbitrary", "arbitrary")),
        name="input_proj",
    )(x_all, norm_g.reshape(1, D_MODEL), mod, w_main, w_small)


DA_QB = 256


def _rope_tables():
    t = jnp.arange(DEC_SEQ)
    nf = DA_DQ // 4
    freqs = ROPE_BASE ** (-jnp.arange(nf, dtype=F32) / nf)
    lane = np.arange(LANES)
    m = lane % DA_DQ
    use_col = (m // (DA_DQ // 2)) == 1
    within = m % (DA_DQ // 2)
    fi = within % nf
    second = within >= nf
    pos = jnp.where(jnp.asarray(use_col)[None, :], (t % GRID_W)[:, None], (t // GRID_W)[:, None]).astype(F32)
    ang = pos * freqs[jnp.asarray(fi)][None, :]
    cos = jnp.cos(ang)
    sin = jnp.sin(ang)
    sec = jnp.asarray(second)[None, :]
    sin_next = jnp.where(sec, 0.0, -sin)
    sin_prev = jnp.where(sec, sin, 0.0)
    return cos.astype(F32), sin_next.astype(F32), sin_prev.astype(F32)


def _rope(x, cos, sin_next, sin_prev):
    nf = DA_DQ // 4
    return (x * cos + pltpu.roll(x, LANES - nf, 1) * sin_next + pltpu.roll(x, nf, 1) * sin_prev)


def _softmax_rows(s):
    m = jnp.max(s, axis=-1, keepdims=True)
    e = jnp.exp(s - m)
    return e / jnp.sum(e, axis=-1, keepdims=True)


_NT = (((1,), (1,)), ((), ()))


def _da_kernel(*refs, seq, ctx_len, lam_init):
    if ctx_len:
        (q_ref, k_ref, v_ref, kc_ref, vc_ref, cos_ref, sn_ref, sp_ref, lam_ref, g_ref, _, o_ref,
         k_scr, v_scr) = refs
    else:
        q_ref, k_ref, v_ref, lam_ref, g_ref, o_ref, k_scr, v_scr = refs
    lam_vec = lam_ref[...]
    lam = (jnp.exp(jnp.sum(lam_vec[0:1, :] * lam_vec[1:2, :], axis=-1, keepdims=True))
           - jnp.exp(jnp.sum(lam_vec[2:3, :] * lam_vec[3:4, :], axis=-1, keepdims=True)) + lam_init)
    if ctx_len:
        k_scr[0:seq, :] = _rope(k_ref[...], cos_ref[...], sn_ref[...], sp_ref[...]).astype(BF16)
        k_scr[seq:seq + ctx_len, :] = kc_ref[...].astype(BF16)
        v_scr[0:seq, :] = v_ref[...].astype(BF16)
        v_scr[seq:seq + ctx_len, :] = vc_ref[...].astype(BF16)
    else:
        k_scr[...] = k_ref[...].astype(BF16)
        v_scr[...] = v_ref[...].astype(BF16)
    lane = lax.broadcasted_iota(jnp.int32, (DA_QB, LANES), 1)
    first_map = lane < DA_DQ
    scale = DA_DQ ** -0.5
    gain = g_ref[...] * (1.0 - lam_init)

    def body(qi, carry):
        rows = pl.ds(pl.multiple_of(qi * DA_QB, DA_QB), DA_QB)
        q = q_ref[rows, :]
        if ctx_len:
            q = _rope(q, cos_ref[rows, :], sn_ref[rows, :], sp_ref[rows, :])
        q = q * scale
        q1 = jnp.where(first_map, q, 0.0).astype(BF16)
        q2 = jnp.where(first_map, 0.0, q).astype(BF16)
        kk = k_scr[...]
        p1 = _softmax_rows(lax.dot_general(q1, kk, _NT, preferred_element_type=F32))
        p2 = _softmax_rows(lax.dot_general(q2, kk, _NT, preferred_element_type=F32))
        p = (p1 - lam * p2).astype(BF16)
        o = jnp.dot(p, v_scr[...], preferred_element_type=F32)
        ms = jnp.mean(o * o, axis=-1, keepdims=True)
        o_ref[rows, :] = (o * lax.rsqrt(ms + EPS)) * gain
        return carry

    lax.fori_loop(0, seq // DA_QB, body, 0)


def _diff_attention_prompt(proj, lam_vec, norm_g, lam_init):
    kern = functools.partial(_da_kernel, seq=SEQ, ctx_len=0, lam_init=lam_init)
    blk = lambda c0: pl.BlockSpec((SEQ, LANES), lambda b, h: (b, c0 + h))
    return pl.pallas_call(
        kern,
        out_shape=jax.ShapeDtypeStruct((M_ALL, GROUP_W), F32),
        grid=(BATCH, DA_HEADS),
        in_specs=[blk(COL_DA_Q), blk(COL_DA_K), blk(COL_DA_V),
                  pl.BlockSpec((4, DA_DQ), lambda b, h: (0, 0)),
                  pl.BlockSpec((1, DA_DV), lambda b, h: (0, 0))],
        out_specs=pl.BlockSpec((SEQ, LANES), lambda b, h: (b, h)),
        scratch_shapes=[pltpu.VMEM((SEQ, LANES), BF16), pltpu.VMEM((SEQ, LANES), BF16)],
        compiler_params=_cparams(("arbitrary", "arbitrary")),
        name="diff_attn_prompt",
    )(proj, proj, proj, lam_vec, norm_g.reshape(1, DA_DV))


def _diff_attention_sample(proj, cache_k, cache_v, layer, rope_tabs, lam_vec, norm_g, lam_init, out_buf):
    kern = functools.partial(_da_kernel, seq=DEC_SEQ, ctx_len=PAST_LEN, lam_init=lam_init)
    off = M_PROMPT // DEC_SEQ
    blk = lambda c0: pl.BlockSpec((DEC_SEQ, LANES), lambda b, h: (off + b, c0 + h))
    cblk = pl.BlockSpec((None, None, PAST_LEN, LANES), lambda b, h: (b, layer, 0, h))
    tab = pl.BlockSpec((DEC_SEQ, LANES), lambda b, h: (0, 0))
    cos, sn, sp = rope_tabs
    return pl.pallas_call(
        kern,
        out_shape=jax.ShapeDtypeStruct((M_ALL, GROUP_W), F32),
        grid=(DEC_BATCH, DA_HEADS),
        in_specs=[blk(COL_DA_Q), blk(COL_DA_K), blk(COL_DA_V), cblk, cblk, tab, tab, tab,
                  pl.BlockSpec((4, DA_DQ), lambda b, h: (0, 0)),
                  pl.BlockSpec((1, DA_DV), lambda b, h: (0, 0)),
                  pl.BlockSpec(memory_space=pl.ANY)],
        out_specs=pl.BlockSpec((DEC_SEQ, LANES), lambda b, h: (off + b, h)),
        scratch_shapes=[pltpu.VMEM((DEC_SEQ + PAST_LEN, LANES), BF16),
                        pltpu.VMEM((DEC_SEQ + PAST_LEN, LANES), BF16)],
        input_output_aliases={10: 0},
        compiler_params=_cparams(("arbitrary", "arbitrary")),
        name="diff_attn_sample",
    )(proj, proj, proj, cache_k, cache_v, cos, sn, sp, lam_vec, norm_g.reshape(1, DA_DV), out_buf)


def _sm_attn_kernel(q_ref, k_ref, v_ref, o_ref):
    q = (q_ref[...] * (NA_DH ** -0.5)).astype(BF16)
    p = _softmax_rows(lax.dot_general(q, k_ref[...].astype(BF16), _NT, preferred_element_type=F32))
    o_ref[...] = jnp.dot(p.astype(BF16), v_ref[...].astype(BF16), preferred_element_type=F32)


def _softmax_attention_prompt(proj):
    blk = lambda c0: pl.BlockSpec((SEQ, LANES), lambda b, h: (b, c0 + h))
    return pl.pallas_call(
        _sm_attn_kernel,
        out_shape=jax.ShapeDtypeStruct((M_ALL, GROUP_W), F32),
        grid=(BATCH, NA_HEADS),
        in_specs=[blk(COL_NA_Q), blk(COL_NA_K), blk(COL_NA_V)],
        out_specs=pl.BlockSpec((SEQ, LANES), lambda b, h: (b, h)),
        compiler_params=_cparams(("arbitrary", "arbitrary")),
        name="softmax_attn_prompt",
    )(proj, proj, proj)


GRID_ROWS = DEC_SEQ // GRID_W
NA_WIN_KEYS = WIN_R * GRID_W


def _na_bias_table(rpb):
    r = np.arange(GRID_ROWS)
    rs = np.clip(r - WIN_R // 2, 0, GRID_ROWS - WIN_R)
    row_idx = rs[:, None] + np.arange(WIN_R)[None, :] - r[:, None] + WIN_R - 1
    qc = np.arange(GRID_W)[:, None]
    kc = np.arange(GRID_W)[None, :]
    win_start = np.clip(qc - WIN_C // 2, 0, GRID_W - WIN_C)
    allowed = (kc >= win_start) & (kc < win_start + WIN_C)
    col_idx = np.clip(kc - qc + WIN_C - 1, 0, 2 * WIN_C - 2)
    t = rpb[:, jnp.asarray(row_idx)][..., jnp.asarray(col_idx)]
    t = jnp.where(jnp.asarray(allowed)[None, None, None], t, NEG_INF)
    return t.transpose(0, 1, 3, 2, 4).reshape(NA_HEADS, GRID_ROWS, GRID_W, NA_WIN_KEYS).astype(F32)


def _na_kernel(q_ref, k_ref, v_ref, kc_ref, vc_ref, bias_ref, _, o_ref):
    kc = kc_ref[...].astype(BF16)
    vc = vc_ref[...].astype(BF16)
    scale = NA_DH ** -0.5
    for r in range(GRID_ROWS):
        rs = min(max(r - WIN_R // 2, 0), GRID_ROWS - WIN_R)
        q = (q_ref[r * GRID_W:(r + 1) * GRID_W, :] * scale).astype(BF16)
        kw = k_ref[rs * GRID_W:(rs + WIN_R) * GRID_W, :].astype(BF16)
        vw = v_ref[rs * GRID_W:(rs + WIN_R) * GRID_W, :].astype(BF16)
        s_win = lax.dot_general(q, kw, _NT, preferred_element_type=F32) + bias_ref[r]
        s_ctx = lax.dot_general(q, kc, _NT, preferred_element_type=F32)
        m = jnp.maximum(jnp.max(s_win, axis=-1, keepdims=True), jnp.max(s_ctx, axis=-1, keepdims=True))
        e_win = jnp.exp(s_win - m)
        e_ctx = jnp.exp(s_ctx - m)
        inv = 1.0 / (jnp.sum(e_win, axis=-1, keepdims=True) + jnp.sum(e_ctx, axis=-1, keepdims=True))
        o = (jnp.dot((e_win * inv).astype(BF16), vw, preferred_element_type=F32)
             + jnp.dot((e_ctx * inv).astype(BF16), vc, preferred_element_type=F32))
        o_ref[r * GRID_W:(r + 1) * GRID_W, :] = o


def _neighborhood_attention_sample(proj, cache_k, cache_v, layer, bias_tab, out_buf):
    off = M_PROMPT // DEC_SEQ
    blk = lambda c0: pl.BlockSpec((DEC_SEQ, LANES), lambda b, h: (off + b, c0 + h))
    cblk = pl.BlockSpec((None, None, PAST_LEN, LANES), lambda b, h: (b, layer, 0, h))
    return pl.pallas_call(
        _na_kernel,
        out_shape=jax.ShapeDtypeStruct((M_ALL, GROUP_W), F32),
        grid=(DEC_BATCH, NA_HEADS),
        in_specs=[blk(COL_NA_Q), blk(COL_NA_K), blk(COL_NA_V), cblk, cblk,
                  pl.BlockSpec((None, GRID_ROWS, GRID_W, NA_WIN_KEYS), lambda b, h: (h, 0, 0, 0)),
                  pl.BlockSpec(memory_space=pl.ANY)],
        out_specs=pl.BlockSpec((DEC_SEQ, LANES), lambda b, h: (off + b, h)),
        input_output_aliases={6: 0},
        compiler_params=_cparams(("arbitrary", "arbitrary")),
        name="nbr_attn_sample",
    )(proj, proj, proj, cache_k, cache_v, bias_tab, out_buf)


TRI_G = 16
_BATCHED_NN = (((2,), (1,)), ((0,), (0,)))


def _tri_inv_kernel(a_ref, o_ref):
    b = -a_ref[...]
    q = b
    for _ in range(5):
        bb = b.astype(BF16)
        b = lax.dot_general(bb, bb, _BATCHED_NN, preferred_element_type=F32)
        q = q + b + lax.dot_general(q.astype(BF16), b.astype(BF16), _BATCHED_NN, preferred_element_type=F32)
    rows = lax.broadcasted_iota(jnp.int32, (TRI_G, CHUNK, CHUNK), 1)
    cols = lax.broadcasted_iota(jnp.int32, (TRI_G, CHUNK, CHUNK), 2)
    o_ref[...] = q + jnp.where(rows == cols, 1.0, 0.0)


def _unit_lower_inverse(a):
    shape = a.shape
    flat = a.reshape(-1, CHUNK, CHUNK)
    n = flat.shape[0]
    out = pl.pallas_call(
        _tri_inv_kernel,
        out_shape=jax.ShapeDtypeStruct((n, CHUNK, CHUNK), F32),
        grid=(n // TRI_G,),
        in_specs=[pl.BlockSpec((TRI_G, CHUNK, CHUNK), lambda i: (i, 0, 0))],
        out_specs=pl.BlockSpec((TRI_G, CHUNK, CHUNK), lambda i: (i, 0, 0)),
        compiler_params=_cparams(("arbitrary",)),
        name="unit_lower_inverse",
    )(flat)
    return out.reshape(shape)


K2_TM = 256


def _split_bf16(a):
    hi = a.astype(BF16)
    lo = (a - hi.astype(F32)).astype(BF16)
    return hi, lo


def _k2_kernel(oa_ref, ob_ref, oc_ref, od_ref, x_ref, w_ref, mod_ref, g_ref, wr_hi_ref, wr_lo_ref, br_ref,
               xo_ref, h2_ref, lg_ref):
    acc = jnp.dot(oa_ref[...].astype(BF16), w_ref[0 * GROUP_W:1 * GROUP_W, :], preferred_element_type=F32)
    acc += jnp.dot(ob_ref[...].astype(BF16), w_ref[1 * GROUP_W:2 * GROUP_W, :], preferred_element_type=F32)
    acc += jnp.dot(oc_ref[...].astype(BF16), w_ref[2 * GROUP_W:3 * GROUP_W, :], preferred_element_type=F32)
    acc += jnp.dot(od_ref[...].astype(BF16), w_ref[3 * GROUP_W:4 * GROUP_W, :], preferred_element_type=F32)
    x = x_ref[...] + mod_ref[2:3, :] * acc
    xo_ref[...] = x
    h2 = _modulated_norm(x, g_ref[...], mod_ref[4:5, :], mod_ref[3:4, :])
    hi, lo = _split_bf16(h2)
    h2_ref[...] = h2
    lg = (jnp.dot(hi, wr_hi_ref[...], preferred_element_type=F32)
          + jnp.dot(hi, wr_lo_ref[...], preferred_element_type=F32)
          + jnp.dot(lo, wr_hi_ref[...], preferred_element_type=F32))
    lg_ref[...] = lg + br_ref[...]


def _output_projection(oa, ob, oc, od, x_all, w_out_bf16, mod, norm2_g, wr_hi, wr_lo, br):
    mix = pl.BlockSpec((K2_TM, GROUP_W), lambda i: (i, 0))
    row = pl.BlockSpec((K2_TM, D_MODEL), lambda i: (i, 0))
    const = lambda shape: pl.BlockSpec(shape, lambda i: (0,) * len(shape))
    return pl.pallas_call(
        _k2_kernel,
        out_shape=(jax.ShapeDtypeStruct((M_ALL, D_MODEL), F32),
                   jax.ShapeDtypeStruct((M_ALL, D_MODEL), F32),
                   jax.ShapeDtypeStruct((M_ALL, LANES), F32)),
        grid=(M_ALL // K2_TM,),
        in_specs=[mix, mix, mix, mix, row, const((D_MODEL, D_MODEL)),
                  pl.BlockSpec((None, 6, D_MODEL), lambda i: (_seg_of_tile(i, K2_TM), 0, 0)),
                  const((1, D_MODEL)), const((D_MODEL, LANES)), const((D_MODEL, LANES)), const((1, LANES))],
        out_specs=(row, row, pl.BlockSpec((K2_TM, LANES), lambda i: (i, 0))),
        compiler_params=_cparams(("arbitrary",)),
        name="output_proj",
    )(oa, ob, oc, od, x_all, w_out_bf16, mod, norm2_g.reshape(1, D_MODEL), wr_hi, wr_lo, br)


MOE_TB = 128
N_ASSIGN = M_ALL * TOP_K
MOE_NBLK = N_ASSIGN // MOE_TB + N_EXPERTS


def _moe_kernel(be_ref, nused_ref, x_ref, w1_ref, w3_ref, w2_ref, y_ref, w1_scr, w3_scr, w2_scr):
    i = pl.program_id(0)
    prev = be_ref[jnp.maximum(i - 1, 0)]
    new_expert = jnp.logical_or(i == 0, be_ref[i] != prev)

    @pl.when(new_expert)
    def _():
        w1_scr[...] = w1_ref[...].astype(BF16)
        w3_scr[...] = w3_ref[...].astype(BF16)
        w2_scr[...] = w2_ref[...].astype(BF16)

    @pl.when(i < nused_ref[0])
    def _():
        x = x_ref[...].astype(BF16)
        a = jnp.dot(x, w1_scr[...], preferred_element_type=F32)
        b = jnp.dot(x, w3_scr[...], preferred_element_type=F32)
        hmid = (a * jax.nn.sigmoid(a) * b).astype(BF16)
        y_ref[...] = jnp.dot(hmid, w2_scr[...], preferred_element_type=F32)

    @pl.when(i >= nused_ref[0])
    def _():
        y_ref[...] = jnp.zeros_like(y_ref)


def _moe_experts(blk_expert, n_used, xb, w1, w3, w2, layer):
    grid_spec = pltpu.PrefetchScalarGridSpec(
        num_scalar_prefetch=2,
        grid=(MOE_NBLK,),
        in_specs=[pl.BlockSpec((MOE_TB, D_MODEL), lambda i, be, nu: (i, 0)),
                  pl.BlockSpec((None, None, D_MODEL, D_FF_EXPERT), lambda i, be, nu: (layer, be[i], 0, 0)),
                  pl.BlockSpec((None, None, D_MODEL, D_FF_EXPERT), lambda i, be, nu: (layer, be[i], 0, 0)),
                  pl.BlockSpec((None, None, D_FF_EXPERT, D_MODEL), lambda i, be, nu: (layer, be[i], 0, 0))],
        out_specs=pl.BlockSpec((MOE_TB, D_MODEL), lambda i, be, nu: (i, 0)),
        scratch_shapes=[pltpu.VMEM((D_MODEL, D_FF_EXPERT), BF16),
                        pltpu.VMEM((D_MODEL, D_FF_EXPERT), BF16),
                        pltpu.VMEM((D_FF_EXPERT, D_MODEL), BF16)])
    return pl.pallas_call(
        _moe_kernel,
        out_shape=jax.ShapeDtypeStruct((MOE_NBLK * MOE_TB, D_MODEL), F32),
        grid_spec=grid_spec,
        compiler_params=_cparams(("arbitrary",)),
        name="moe_experts",
    )(blk_expert, n_used, xb, w1, w3, w2)


def _route(logits):
    g_logits = logits[:, :N_GROUPS]
    grp = jnp.argmax(g_logits, axis=-1)
    p_grp = jnp.take_along_axis(jax.nn.softmax(g_logits, axis=-1), grp[:, None], axis=-1)
    e_logits = logits[:, N_GROUPS:N_GROUPS + N_EXPERTS].reshape(M_ALL, N_GROUPS, E_PER_GROUP)
    e_logits = jnp.take_along_axis(e_logits, grp[:, None, None], axis=1)[:, 0]
    top_v, top_i = lax.top_k(e_logits, TOP_K)
    gates = jax.nn.softmax(top_v, axis=-1) * p_grp
    expert_id = (grp[:, None] * E_PER_GROUP + top_i).astype(jnp.int32)
    return expert_id, gates


def _dispatch_plan(expert_id):
    flat = expert_id.reshape(-1)
    order = jnp.argsort(flat)
    sorted_e = flat[order]
    counts = jnp.zeros((N_EXPERTS,), jnp.int32).at[flat].add(1)
    padded = (counts + MOE_TB - 1) // MOE_TB * MOE_TB
    pad_end = jnp.cumsum(padded)
    pad_start = pad_end - padded
    start = jnp.cumsum(counts) - counts
    dest_sorted = pad_start[sorted_e] + jnp.arange(N_ASSIGN, dtype=jnp.int32) - start[sorted_e]
    slot_tok = jnp.zeros((MOE_NBLK * MOE_TB,), jnp.int32).at[dest_sorted].set((order // TOP_K).astype(jnp.int32))
    n_used = (pad_end[-1] // MOE_TB).astype(jnp.int32)
    blk_start = jnp.minimum(jnp.arange(MOE_NBLK, dtype=jnp.int32), n_used - 1) * MOE_TB
    blk_expert = jnp.minimum(jnp.searchsorted(pad_end, blk_start, side='right'), N_EXPERTS - 1).astype(jnp.int32)
    dest = jnp.zeros((N_ASSIGN,), jnp.int32).at[order].set(dest_sorted)
    return slot_tok, blk_expert, n_used.reshape(1), dest


CB_TM = 512


def _combine_kernel(x_ref, y0_ref, y1_ref, gate_ref, mod_ref, fg_ref, xo_ref, *maybe_norm_ref):
    gates = gate_ref[...]
    y = y0_ref[...] * gates[:, 0:1] + y1_ref[...] * gates[:, 1:2]
    x = x_ref[...] + mod_ref[5:6, :] * y
    xo_ref[...] = x
    if maybe_norm_ref:
        ms = jnp.mean(x * x, axis=-1, keepdims=True)
        maybe_norm_ref[0][...] = (x * lax.rsqrt(ms + EPS)) * fg_ref[...]


def _combine(x_all, y0, y1, gates, mod, final_g, with_final_norm):
    row = pl.BlockSpec((CB_TM, D_MODEL), lambda i: (i, 0))
    out_shape = [jax.ShapeDtypeStruct((M_ALL, D_MODEL), F32)]
    out_specs = [row]
    if with_final_norm:
        out_shape.append(jax.ShapeDtypeStruct((M_ALL, D_MODEL), F32))
        out_specs.append(row)
    return pl.pallas_call(
        _combine_kernel,
        out_shape=tuple(out_shape),
        grid=(M_ALL // CB_TM,),
        in_specs=[row, row, row,
                  pl.BlockSpec((CB_TM, TOP_K), lambda i: (i, 0)),
                  pl.BlockSpec((None, 6, D_MODEL), lambda i: (_seg_of_tile(i, CB_TM), 0, 0)),
                  pl.BlockSpec((1, D_MODEL), lambda i: (0, 0))],
        out_specs=tuple(out_specs),
        compiler_params=_cparams(("arbitrary",)),
        name="moe_combine_final" if with_final_norm else "moe_combine",
    )(x_all, y0, y1, gates, mod, final_g.reshape(1, D_MODEL))


def _rms_norm(x, g):
    xf = x.astype(F32)
    y = xf * lax.rsqrt(jnp.mean(xf * xf, axis=-1, keepdims=True) + EPS)
    return y.astype(x.dtype) * g


def _l2_normalize(x):
    xf = x.astype(F32)
    return xf * lax.rsqrt(jnp.sum(xf * xf, axis=-1, keepdims=True) + EPS)


def _depthwise_conv(x, w):
    k = w.shape[0]
    return lax.conv_general_dilated(x, w[:, None, :], (1,), ((k // 2, k // 2),),
                                    dimension_numbers=('NWC', 'WIO', 'NWC'),
                                    feature_group_count=x.shape[-1])


def _conv_module(u, dw_w, dw_b, gn_g, gn_b, pw_w, pw_b):
    a, gate = jnp.split(u, 2, axis=-1)
    h = _depthwise_conv(a * jax.nn.sigmoid(gate), dw_w) + dw_b
    B, L, C = h.shape
    hf = h.astype(F32).reshape(B, L, CONV_GROUPS, C // CONV_GROUPS)
    mu = jnp.mean(hf, axis=-1, keepdims=True)
    var = jnp.mean(jnp.square(hf - mu), axis=-1, keepdims=True)
    hn = ((hf - mu) * lax.rsqrt(var + EPS)).reshape(B, L, C).astype(h.dtype) * gn_g + gn_b
    return jax.nn.silu(hn) @ pw_w + pw_b


def _gated_delta_chunked(q, k, v, g, beta, s0, reverse=False):
    B, L, H, dk = q.shape
    dv = v.shape[-1]
    n = L // CHUNK
    qc = (q.astype(F32) * dk ** -0.5).reshape(B, n, CHUNK, H, dk).transpose(1, 0, 3, 2, 4)
    kc = k.astype(F32).reshape(B, n, CHUNK, H, dk).transpose(1, 0, 3, 2, 4)
    vc = v.astype(F32).reshape(B, n, CHUNK, H, dv).transpose(1, 0, 3, 2, 4)
    gc = lax.cumsum(g.astype(F32).reshape(B, n, CHUNK, H).transpose(1, 0, 3, 2), axis=3, reverse=reverse)
    bc = beta.astype(F32).reshape(B, n, CHUNK, H).transpose(1, 0, 3, 2)
    ones = jnp.ones((CHUNK, CHUNK), bool)
    causal = jnp.triu(ones) if reverse else jnp.tril(ones)
    strict = jnp.triu(ones, 1) if reverse else jnp.tril(ones, -1)
    decay = jnp.where(causal, jnp.exp(jnp.where(causal, gc[..., :, None] - gc[..., None, :], 0.0)), 0.0)
    kb = kc * bc[..., None]
    a = jnp.where(strict, jnp.einsum('nbhid,nbhjd->nbhij', kb, kc) * decay, 0.0)
    t_inv = _unit_lower_inverse(a)
    u =jnp.einsum('nbhij,nbhjd->nbhid', t_inv, vc * bc[..., None])
    w = jnp.einsum('nbhij,nbhjd->nbhid', t_inv, kb * jnp.exp(gc)[..., None])
    a_qk = jnp.where(causal, jnp.einsum('nbhid,nbhjd->nbhij', qc, kc) * decay, 0.0)

    def step(s, xs):
        qi, ki, ui, wi, gi, ai = xs
        v_new = ui - jnp.einsum('bhcd,bhde->bhce', wi, s)
        o = (jnp.einsum('bhcd,bhde->bhce', qi * jnp.exp(gi)[..., None], s)
             + jnp.einsum('bhij,bhje->bhie', ai, v_new))
        g_last = gi[..., :1] if reverse else gi[..., -1:]
        s = (s * jnp.exp(g_last)[..., None]
             + jnp.einsum('bhcd,bhce->bhde', ki * jnp.exp(g_last - gi)[..., None], v_new))
        return s, o

    s_fin, o = lax.scan(step, s0.astype(F32), (qc, kc, u, w, gc, a_qk), reverse=reverse)
    return o.transpose(1, 0, 3, 2, 4).reshape(B, L, H, dv), s_fin


def _deltanet_mixer(qkv, z, b_fb, a_fb, conv_w, a_log, dt_bias, norm_g, s0_f, s0_b):
    B, L, _ = qkv.shape
    qkv = jax.nn.silu(_depthwise_conv(qkv, conv_w))
    q, k, v = jnp.split(qkv, 3, axis=-1)
    q = _l2_normalize(q.reshape(B, L, DN_HEADS, DN_DK))
    k = _l2_normalize(k.reshape(B, L, DN_HEADS, DN_DK))
    v = v.reshape(B, L, DN_HEADS, DN_DV)
    beta = jax.nn.sigmoid(b_fb.astype(F32)).reshape(B, L, 2, DN_HEADS)
    g = -jnp.exp(a_log.astype(F32)) * jax.nn.softplus(a_fb.astype(F32).reshape(B, L, 2, DN_HEADS) + dt_bias)
    o_f, s_f = _gated_delta_chunked(q, k, v, g[:, :, 0], beta[:, :, 0], s0_f)
    o_b, s_b = _gated_delta_chunked(q, k, v, g[:, :, 1], beta[:, :, 1], s0_b, reverse=True)
    o = o_f + o_b
    o = _rms_norm(o, norm_g) * jax.nn.silu(z.astype(F32).reshape(B, L, DN_HEADS, DN_DV))
    return o.reshape(B, L, GROUP_W).astype(qkv.dtype), s_f, s_b


def kernel(x_prompt, x_sample, cache_da_k, cache_da_v, cache_na_k, cache_na_v, state_dn_fwd, state_dn_bwd,
           c, c_ctx, w_in, w_out, ada_w, ada_b, norm1_g, norm2_g, da_lam_q1, da_lam_k1, da_lam_q2, da_lam_k2,
           da_norm_g, cv_dw_w, cv_dw_b, cv_gn_g, cv_gn_b, cv_pw_w, cv_pw_b, na_rpb, dn_conv_w, dn_a_log,
           dn_dt_bias, dn_norm_g, moe_wg, moe_bg, moe_we, moe_be, moe_w1, moe_w3, moe_w2, final_norm_g):
    x_all = jnp.concatenate([x_prompt.reshape(M_PROMPT, D_MODEL), x_sample.reshape(M_SAMPLE, D_MODEL)], axis=0)
    cvec16 = jnp.concatenate([c_ctx[None, :], c, jnp.zeros((16 - N_SEG, D_MODEL), F32)], axis=0)
    mod_all = _ada_all(cvec16, ada_w, ada_b)[:, :N_SEG].reshape(DEPTH, N_SEG, 6, D_MODEL)
    rope_tabs = _rope_tables()
    cache_da_k2 = cache_da_k.reshape(DEC_BATCH, DEPTH, PAST_LEN, GROUP_W)
    cache_da_v2 = cache_da_v.reshape(DEC_BATCH, DEPTH, PAST_LEN, GROUP_W)
    cache_na_k2 = cache_na_k.reshape(DEC_BATCH, DEPTH, PAST_LEN, GROUP_W)
    cache_na_v2 = cache_na_v.reshape(DEC_BATCH, DEPTH, PAST_LEN, GROUP_W)

    new_da_k, new_da_v, new_na_k, new_na_v, new_sf, new_sb = [], [], [], [], [], []
    y_final = None
    for l in range(DEPTH):
        mod = mod_all[l]
        w_main = w_in[l, :, :N_MAIN].astype(BF16)
        w_small = jnp.pad(w_in[l, :, N_MAIN:], ((0, 0), (0, LANES - N_SMALL))).astype(BF16)
        proj, proj_small = _input_projection(x_all, norm1_g[l], mod, w_main, w_small)

        pp = proj[:M_PROMPT]
        new_da_k.append(pp[:, 512:1024].reshape(BATCH, SEQ, DA_HEADS, 2 * DA_DQ))
        new_da_v.append(pp[:, 1024:1536].reshape(BATCH, SEQ, DA_HEADS, DA_DV))
        new_na_k.append(pp[:, 3072:3584].reshape(BATCH, SEQ, NA_HEADS, NA_DH))
        new_na_v.append(pp[:, 3584:4096].reshape(BATCH, SEQ, NA_HEADS, NA_DH))

        lam_init = 0.8 - 0.6 * math.exp(-0.3 * l)
        lam_vec = jnp.stack([da_lam_q1[l], da_lam_k1[l], da_lam_q2[l], da_lam_k2[l]], axis=0)
        o_a = _diff_attention_prompt(proj, lam_vec, da_norm_g[l], lam_init)
        o_a = _diff_attention_sample(proj, cache_da_k2, cache_da_v2, l, rope_tabs, lam_vec, da_norm_g[l],
                                     lam_init, o_a)

        ps = proj[M_PROMPT:]
        cv = lambda u: _conv_module(u, cv_dw_w[l], cv_dw_b[l], cv_gn_g[l], cv_gn_b[l], cv_pw_w[l], cv_pw_b[l])
        o_b = jnp.concatenate([cv(pp[:, 1536:2560].reshape(BATCH, SEQ, 2 * GROUP_W)).reshape(M_PROMPT, GROUP_W),
                               cv(ps[:, 1536:2560].reshape(DEC_BATCH, DEC_SEQ, 2 * GROUP_W)).reshape(M_SAMPLE, GROUP_W)],
                              axis=0)

        o_c = _softmax_attention_prompt(proj)
        o_c = _neighborhood_attention_sample(proj, cache_na_k2, cache_na_v2, l, _na_bias_table(na_rpb[l]), o_c)

        small_p = proj_small[:M_PROMPT]
        small_s = proj_small[M_PROMPT:]
        zeros_state = jnp.zeros((BATCH, DN_HEADS, DN_DK, DN_DV), F32)
        o_d_p, s_f, s_b = _deltanet_mixer(
            pp[:, 4096:5632].reshape(BATCH, SEQ, 3 * GROUP_W), pp[:, 5632:6144].reshape(BATCH, SEQ, GROUP_W),
            small_p[:, 0:8].reshape(BATCH, SEQ, 8), small_p[:, 8:16].reshape(BATCH, SEQ, 8),
            dn_conv_w[l], dn_a_log[l], dn_dt_bias[l], dn_norm_g[l], zeros_state, zeros_state)
        new_sf.append(s_f)
        new_sb.append(s_b)
        o_d_s, _, _ = _deltanet_mixer(
            ps[:, 4096:5632].reshape(DEC_BATCH, DEC_SEQ, 3 * GROUP_W),
            ps[:, 5632:6144].reshape(DEC_BATCH, DEC_SEQ, GROUP_W),
            small_s[:, 0:8].reshape(DEC_BATCH, DEC_SEQ, 8), small_s[:, 8:16].reshape(DEC_BATCH, DEC_SEQ, 8),
            dn_conv_w[l], dn_a_log[l], dn_dt_bias[l], dn_norm_g[l], state_dn_fwd[:, l], state_dn_bwd[:, l])
        o_d = jnp.concatenate([o_d_p.reshape(M_PROMPT, GROUP_W), o_d_s.reshape(M_SAMPLE, GROUP_W)], axis=0)

        w_r = jnp.pad(jnp.concatenate([moe_wg[l], moe_we[l]], axis=1),
                      ((0, 0), (0, LANES - N_GROUPS - N_EXPERTS)))
        wr_hi, wr_lo = _split_bf16(w_r)
        b_r = jnp.pad(jnp.concatenate([moe_bg[l], moe_be[l]]), (0, LANES - N_GROUPS - N_EXPERTS)).reshape(1, LANES)
        x_all, h2, logits = _output_projection(o_a, o_b, o_c, o_d, x_all, w_out[l].astype(BF16), mod,
                                               norm2_g[l], wr_hi, wr_lo, b_r)

        expert_id, gates = _route(logits)
        slot_tok, blk_expert, n_used, dest = _dispatch_plan(expert_id)
        xb = h2[slot_tok]
        yb = _moe_experts(blk_expert, n_used, xb, moe_w1, moe_w3, moe_w2, l)
        dest2 = dest.reshape(M_ALL, TOP_K)
        last = l == DEPTH - 1
        res = _combine(x_all, yb[dest2[:, 0]], yb[dest2[:, 1]], gates, mod, final_norm_g, last)
        x_all = res[0]
        if last:
            y_final = res[1]

    y_prompt = y_final[:M_PROMPT].reshape(BATCH, SEQ, D_MODEL)
    y_sample = y_final[M_PROMPT:].reshape(DEC_BATCH, DEC_SEQ, D_MODEL)
    return (y_prompt, y_sample,
            jnp.stack(new_da_k, axis=1), jnp.stack(new_da_v, axis=1),
            jnp.stack(new_na_k, axis=1), jnp.stack(new_na_v, axis=1),
            jnp.stack(new_sf, axis=1), jnp.stack(new_sb, axis=1))
```
